```python
import math
import jax, jax.numpy as jnp
from jax import lax
import numpy as np

D_MODEL = 2048
BATCH = 8
SEQ = 2048
DEPTH = 2

D_MIX = D_MODEL
D_LRU = D_MIX // 2
N_LRU_BLOCKS = 8
LRU_BLOCK = D_LRU // N_LRU_BLOCKS
CONV_WIDTH = 4
LRU_C = 8.0
D_ATTN = D_MIX - D_LRU
HEAD_DIM = 128
N_HEADS = D_ATTN // HEAD_DIM
DILATED_PAIRS = ((128, 1), (512, 4), (2048, 16))
D_IN = 2 * D_LRU + 3 * D_ATTN
NEG_INF = -1e30
N_BUCKETS = 32
MAX_EXACT = 8
MAX_DISTANCE = 1024
N_EXPERTS = 32
TOP_K = 4
D_FF = D_MODEL
SWIGLU_LIMIT = 7.0
SWIGLU_ALPHA = 1.702
MOE_BLOCK = 256
DEEPNORM_ALPHA = (2 * DEPTH) ** 0.25
DEEPNORM_BETA = (8 * DEPTH) ** -0.25
LN_EPS = 1e-5

kernel_name = "hybrid_rglru_dilated_attn_moe_encoder"


def layer_norm(x, g, b):
    xf = x.astype(jnp.float32)
    mu = jnp.mean(xf, axis=-1, keepdims=True)
    var = jnp.mean(jnp.square(xf - mu), axis=-1, keepdims=True)
    return ((xf - mu) * lax.rsqrt(var + LN_EPS) * g.astype(jnp.float32) + b.astype(jnp.float32)).astype(x.dtype)


def rms_norm(x, g):
    xf = x.astype(jnp.float32)
    return xf * lax.rsqrt(jnp.mean(jnp.square(xf), axis=-1, keepdims=True) + LN_EPS) * g.astype(jnp.float32)


def t5_bucket(rel):
    nb = N_BUCKETS // 2
    ret = jnp.where(rel > 0, nb, 0)
    n = jnp.abs(rel)
    nf = jnp.maximum(n, 1).astype(jnp.float32)
    large = MAX_EXACT + (jnp.log(nf / MAX_EXACT) / math.log(MAX_DISTANCE / MAX_EXACT) * (nb - MAX_EXACT)).astype(jnp.int32)
    large = jnp.minimum(large, nb - 1)
    return ret + jnp.where(n < MAX_EXACT, n, large)


def dilated_branch(q, k, v, rel_bias, window, dilation):
    B, S, H, hd = q.shape
    half = (window // 2) // dilation
    L = S // dilation
    nb = -(-L // half)
    Lp = nb * half

    def to_sub(t):
        return t.reshape(B, L, dilation, H, hd).transpose(0, 3, 2, 1, 4)

    qs, ks, vs = to_sub(q), to_sub(k), to_sub(v)
    qb = jnp.pad(qs, ((0, 0), (0, 0), (0, 0), (0, Lp - L), (0, 0))).reshape(B, H, dilation, nb, half, hd)

    def bands(t):
        tb = jnp.pad(t, ((0, 0), (0, 0), (0, 0), (half, Lp - L + half), (0, 0))).reshape(B, H, dilation, nb + 2, half, hd)
        return jnp.concatenate([tb[:, :, :, :-2], tb[:, :, :, 1:-1], tb[:, :, :, 2:]], axis=4)

    kb, vb = bands(ks), bands(vs)
    qi = jnp.arange(half, dtype=jnp.int32)[:, None]
    kj = jnp.arange(3 * half, dtype=jnp.int32)[None, :]
    step = kj - half - qi
    kpos = jnp.arange(nb, dtype=jnp.int32)[:, None, None] * half + kj[None] - half
    valid = (jnp.abs(step) <= half)[None] & (kpos >= 0) & (kpos < L)
    bias = rel_bias[t5_bucket(step * dilation)].astype(jnp.float32)
    bias = bias.transpose(2, 0, 1)[:, None, None]

    s = jnp.einsum('bhrnqe,bhrnke->bhrnqk', qb, kb) * (hd ** -0.5) + bias
    s = jnp.where(valid, s, NEG_INF)
    m = jnp.max(s, axis=-1)
    p = jnp.exp(s - m[..., None])
    den = jnp.sum(p, axis=-1)
    o = jnp.einsum('bhrnqk,bhrnke->bhrnqe', p, vb)

    def from_sub(t):
        t = t.reshape(B, H, dilation, Lp, *t.shape[5:])[:, :, :, :L]
        t = jnp.swapaxes(jnp.moveaxis(t, 1, 3), 1, 2)
        return t.reshape(B, S, H, *t.shape[4:])

    return from_sub(o), from_sub(m), from_sub(den)


def dilated_attention(q, k, v, rel_bias):
    outs = [dilated_branch(q, k, v, rel_bias, w, d) for (w, d) in DILATED_PAIRS]
    m_all = outs[0][1]
    for _, m_b, _ in outs[1:]:
        m_all = jnp.maximum(m_all, m_b)
    num = 0.0
    den = 0.0
    for o_b, m_b, d_b in outs:
        w_b = jnp.exp(m_b - m_all)
        num = num + w_b[..., None] * o_b
        den = den + w_b * d_b
    return num / den[..., None]


def _linear_combine(e1, e2):
    a1, b1 = e1
    a2, b2 = e2
    return a1 * a2, a2 * b1 + b2


def rg_lru_direction(xc, w_a, b_a, w_x, b_x, lam, reverse):
    B, S, _ = xc.shape
    xg = xc.reshape(B, S, N_LRU_BLOCKS, LRU_BLOCK)
    r = jax.nn.sigmoid(jnp.einsum('bsgi,gij->bsgj', xg, w_a.astype(jnp.float32)).reshape(B, S, D_LRU) + b_a.astype(jnp.float32))
    i = jax.nn.sigmoid(jnp.einsum('bsgi,gij->bsgj', xg, w_x.astype(jnp.float32)).reshape(B, S, D_LRU) + b_x.astype(jnp.float32))
    log_a = -LRU_C * r * jax.nn.softplus(-lam.astype(jnp.float32))
    a = jnp.exp(log_a)
    b = jnp.sqrt(-jnp.expm1(2.0 * log_a)) * (i * xc)
    _, h = lax.associative_scan(_linear_combine, (a, b), axis=1, reverse=reverse)
    return h


def hybrid_mixer(h, rel_bias, w_in, conv_w, conv_b, w_a, b_a, w_x, b_x, lam, g_lru, g_attn, w_out):
    B, S, _ = h.shape
    proj = h @ w_in
    xb, gb, q, k, v = jnp.split(proj, [D_LRU, 2 * D_LRU, 2 * D_LRU + D_ATTN, 2 * D_LRU + 2 * D_ATTN], axis=-1)
    left = CONV_WIDTH // 2
    xc = lax.conv_general_dilated(xb, conv_w[:, None, :], window_strides=(1,),
                                  padding=[(left, CONV_WIDTH - 1 - left)],
                                  dimension_numbers=('NWC', 'WIO', 'NWC'),
                                  feature_group_count=D_LRU) + conv_b
    xc = xc.astype(jnp.float32)
    h_rec = (rg_lru_direction(xc, w_a[0], b_a[0], w_x[0], b_x[0], lam[0], False)
             + rg_lru_direction(xc, w_a[1], b_a[1], w_x[1], b_x[1], lam[1], True))
    lru_out = h_rec * jax.nn.gelu(gb.astype(jnp.float32))
    def heads(t):
        return t.reshape(B, S, N_HEADS, HEAD_DIM).astype(jnp.float32)
    attn_out = dilated_attention(heads(q), heads(k), heads(v), rel_bias).reshape(B, S, D_ATTN)
    merged = jnp.concatenate([rms_norm(lru_out, g_lru), rms_norm(attn_out, g_attn)], axis=-1).astype(h.dtype)
    return merged @ w_out


def moe_ffn(h, w_router, b_router, w_gate_up, b_gate_up, w_down, b_down):
    B, S, D = h.shape
    T = B * S
    xf = h.reshape(T, D)
    logits = (xf @ w_router + b_router).astype(jnp.float32)
    top_v, top_e = lax.top_k(logits, TOP_K)
    top_w = jax.nn.softmax(top_v, axis=-1)
    TK = T * TOP_K
    flat_e = top_e.reshape(TK).astype(jnp.int32)
    flat_tok = jnp.arange(TK, dtype=jnp.int32) // TOP_K
    flat_w = top_w.reshape(TK)
    order = jnp.argsort(flat_e)
    sorted_e = flat_e[order]
    counts = jnp.zeros((N_EXPERTS,), jnp.int32).at[flat_e].add(1)
    padded = (counts + MOE_BLOCK - 1) // MOE_BLOCK * MOE_BLOCK
    start = jnp.cumsum(counts) - counts
    pend = jnp.cumsum(padded)
    pstart = pend - padded
    dest = pstart[sorted_e] + jnp.arange(TK, dtype=jnp.int32) - start[sorted_e]
    n_blocks = -(-TK // MOE_BLOCK) + N_EXPERTS
    P = n_blocks * MOE_BLOCK
    row_tok = jnp.full((P,), T, jnp.int32).at[dest].set(flat_tok[order])
    row_w = jnp.zeros((P,), jnp.float32).at[dest].set(flat_w[order])
    block_e = jnp.searchsorted(pend, jnp.arange(n_blocks, dtype=jnp.int32) * MOE_BLOCK, side='right')
    block_e = jnp.minimum(block_e, N_EXPERTS - 1).astype(jnp.int32)
    x_rows = jnp.concatenate([xf, jnp.zeros((1, D), xf.dtype)], axis=0)[row_tok].reshape(n_blocks, MOE_BLOCK, D)

    def expert_block(args):
        xb, e = args
        gu = xb @ w_gate_up[e] + b_gate_up[e]
        gate = jnp.minimum(gu[:, :D_FF], SWIGLU_LIMIT)
        up = jnp.clip(gu[:, D_FF:], -SWIGLU_LIMIT, SWIGLU_LIMIT)
        glu = gate * jax.nn.sigmoid(gate * SWIGLU_ALPHA)
        return ((up + 1.0) * glu) @ w_down[e] + b_down[e]

    y_rows = lax.map(expert_block, (x_rows, block_e)).reshape(P, D)
    y = jnp.zeros((T + 1, D), y_rows.dtype).at[row_tok].add(y_rows * row_w[:, None].astype(y_rows.dtype))
    return y[:T].reshape(B, S, D)


def setup_inputs(seed: int = 0) -> dict:
    key = jax.random.key(seed)
    ks = jax.random.split(key, 26)
    f32 = jnp.float32
    nrm = lambda k, shape, s: jax.random.normal(k, shape, f32) * s
    w_in = nrm(ks[4], (DEPTH, D_MODEL, D_IN), D_MODEL ** -0.5)
    v_start = 2 * D_LRU + 2 * D_ATTN
    w_in = w_in.at[:, :, v_start:].multiply(DEEPNORM_BETA)
    u = jax.random.uniform(ks[11], (DEPTH, 2, D_LRU), f32, minval=0.9, maxval=0.999)
    sa = u ** (1.0 / LRU_C)
    lru_lambda = jnp.log(sa / (1.0 - sa))
    return {
        "x": nrm(ks[0], (BATCH, SEQ, D_MODEL), 1.0),
        "c": nrm(ks[1], (BATCH, D_MODEL), 1.0),
        "rel_bias": nrm(ks[2], (N_BUCKETS, N_HEADS), 0.5),
        "w_ada": nrm(ks[3], (DEPTH, D_MODEL, 6 * D_MODEL), 0.5 * D_MODEL ** -0.5),
        "b_ada": nrm(ks[5], (DEPTH, 6 * D_MODEL), 0.01),
        "w_in": w_in,
        "conv_w": nrm(ks[6], (DEPTH, CONV_WIDTH, D_LRU), CONV_WIDTH ** -0.5),
        "conv_b": nrm(ks[7], (DEPTH, D_LRU), 0.01),
        "lru_w_a": nrm(ks[8], (DEPTH, 2, N_LRU_BLOCKS, LRU_BLOCK, LRU_BLOCK), LRU_BLOCK ** -0.5),
        "lru_b_a": nrm(ks[9], (DEPTH, 2, D_LRU), 0.01),
        "lru_w_x": nrm(ks[10], (DEPTH, 2, N_LRU_BLOCKS, LRU_BLOCK, LRU_BLOCK), LRU_BLOCK ** -0.5),
        "lru_b_x": nrm(ks[12], (DEPTH, 2, D_LRU), 0.01),
        "lru_lambda": lru_lambda,
        "norm_lru_g": 1.0 + nrm(ks[13], (DEPTH, D_LRU), 0.02),
        "norm_attn_g": 1.0 + nrm(ks[14], (DEPTH, D_ATTN), 0.02),
        "w_out": nrm(ks[15], (DEPTH, D_MIX, D_MODEL), DEEPNORM_BETA * D_MIX ** -0.5),
        "ln_mix_g": 1.0 + nrm(ks[16], (DEPTH, D_MODEL), 0.02),
        "ln_mix_b": nrm(ks[17], (DEPTH, D_MODEL), 0.01),
        "w_router": nrm(ks[18], (DEPTH, D_MODEL, N_EXPERTS), D_MODEL ** -0.5),
        "b_router": nrm(ks[19], (DEPTH, N_EXPERTS), 0.01),
        "w_gate_up": nrm(ks[20], (DEPTH, N_EXPERTS, D_MODEL, 2 * D_FF), DEEPNORM_BETA * D_MODEL ** -0.5),
        "b_gate_up": nrm(ks[21], (DEPTH, N_EXPERTS, 2 * D_FF), 0.01),
        "w_down": nrm(ks[22], (DEPTH, N_EXPERTS, D_FF, D_MODEL), DEEPNORM_BETA * D_FF ** -0.5),
        "b_down": nrm(ks[23], (DEPTH, N_EXPERTS, D_MODEL), 0.01),
        "ln_ffn_g": 1.0 + nrm(ks[24], (DEPTH, D_MODEL), 0.02),
        "ln_ffn_b": nrm(ks[25], (DEPTH, D_MODEL), 0.01),
    }


def reference(x, c, rel_bias, w_ada, b_ada, w_in, conv_w, conv_b, lru_w_a, lru_b_a, lru_w_x, lru_b_x,
              lru_lambda, norm_lru_g, norm_attn_g, w_out, ln_mix_g, ln_mix_b, w_router, b_router,
              w_gate_up, b_gate_up, w_down, b_down, ln_ffn_g, ln_ffn_b):
    c_act = jax.nn.silu(c)
    for l in range(DEPTH):
        mod = c_act @ w_ada[l] + b_ada[l]
        sh_m, sc_m, g_m, sh_f, sc_f, g_f = jnp.split(mod[:, None, :], 6, axis=-1)
        hm = x * (1.0 + sc_m) + sh_m
        y = hybrid_mixer(hm, rel_bias, w_in[l], conv_w[l], conv_b[l], lru_w_a[l], lru_b_a[l],
                         lru_w_x[l], lru_b_x[l], lru_lambda[l], norm_lru_g[l], norm_attn_g[l], w_out[l])
        x = layer_norm(DEEPNORM_ALPHA * x + g_m * y, ln_mix_g[l], ln_mix_b[l])
        hf = x * (1.0 + sc_f) + sh_f
        y = moe_ffn(hf, w_router[l], b_router[l], w_gate_up[l], b_gate_up[l], w_down[l], b_down[l])
        x = layer_norm(DEEPNORM_ALPHA * x + g_f * y, ln_ffn_g[l], ln_ffn_b[l])
    return x
```

```python
import functools
import math

import jax
import jax.numpy as jnp
from jax import lax
from jax.experimental import pallas as pl
from jax.experimental.pallas import tpu as pltpu

f32 = jnp.float32
bf16 = jnp.bfloat16
i32 = jnp.int32
u32 = jnp.uint32

D_MODEL = 2048
DEPTH = 2
D_LRU = 1024
N_LRU_BLOCKS = 8
LRU_BLOCK = 128
CONV_WIDTH = 4
LRU_C = 8.0
D_ATTN = 1024
HEAD_DIM = 128
N_HEADS = 8
D_IN = 2 * D_LRU + 3 * D_ATTN
NEG_INF = -1e30
N_BUCKETS = 32
MAX_EXACT = 8
MAX_DISTANCE = 1024
N_EXPERTS = 32
TOP_K = 4
D_FF = D_MODEL
SWIGLU_LIMIT = 7.0
SWIGLU_ALPHA = 1.702
DEEPNORM_ALPHA = (2 * DEPTH) ** 0.25
LN_EPS = 1e-5

VMEM_LIMIT_BIG = 56 * 1024 * 1024
VMEM_LIMIT_MID = 40 * 1024 * 1024

ROW_TILE = 512
TOK_TILE = 256
EXP_TILE = 512
LRU_CHUNK = 128
Q_BLOCK = 128
HALF = 64
RES16 = 16


def _cparams(sem, vmem=None):
    return pltpu.CompilerParams(dimension_semantics=sem, vmem_limit_bytes=vmem)


def _ada_kernel(c_ref, w_ref, b_ref, o_ref):
    c = c_ref[...]
    ca = c * jax.nn.sigmoid(c)
    o_ref[0] = jnp.dot(ca, w_ref[0], precision=lax.Precision.HIGHEST,
                       preferred_element_type=f32) + b_ref[0]


def _ada_mod(c, w_ada, b_ada):
    depth, d, n = w_ada.shape
    b = c.shape[0]
    tn = 1024
    return pl.pallas_call(
        _ada_kernel,
        out_shape=jax.ShapeDtypeStruct((depth, b, n), f32),
        grid=(depth, n // tn),
        in_specs=[
            pl.BlockSpec((b, d), lambda l, j: (0, 0)),
            pl.BlockSpec((1, d, tn), lambda l, j: (l, 0, j)),
            pl.BlockSpec((1, 1, tn), lambda l, j: (l, 0, j)),
        ],
        out_specs=pl.BlockSpec((1, b, tn), lambda l, j: (l, 0, j)),
        compiler_params=_cparams(("parallel", "parallel"), VMEM_LIMIT_MID),
        name="ada_mod",
    )(c, w_ada, b_ada.reshape(depth, 1, n))


def _inproj_kernel(x_ref, sh_ref, sc_ref, w_ref, o_ref):
    h = x_ref[...] * (1.0 + sc_ref[...]) + sh_ref[...]
    o_ref[...] = jnp.dot(h.astype(bf16), w_ref[...], preferred_element_type=f32).astype(o_ref.dtype)


def _in_proj(x2, mod3, w_in_bf, seq):
    t, d = x2.shape
    n = w_in_bf.shape[1]
    tn = n // 2
    tiles_per_seq = seq // ROW_TILE
    return pl.pallas_call(
        _inproj_kernel,
        out_shape=jax.ShapeDtypeStruct((t, n), bf16),
        grid=(n // tn, t // ROW_TILE),
        in_specs=[
            pl.BlockSpec((ROW_TILE, d), lambda j, i: (i, 0)),
            pl.BlockSpec((None, 1, d), lambda j, i: (i // tiles_per_seq, 0, 0)),
            pl.BlockSpec((None, 1, d), lambda j, i: (i // tiles_per_seq, 0, 1)),
            pl.BlockSpec((d, tn), lambda j, i: (0, j)),
        ],
        out_specs=pl.BlockSpec((ROW_TILE, tn), lambda j, i: (i, j)),
        compiler_params=_cparams(("parallel", "parallel"), VMEM_LIMIT_BIG),
        name="in_proj",
    )(x2, mod3, mod3, w_in_bf)


def _lru_kernel(xb_ref, gb_ref, cw_ref, cb_ref, wg_ref, bg_ref, lam_ref, o_ref,
                xt_ref, hs_ref, a_ref, b_ref):
    nb, seq, _ = xb_ref.shape
    tc = LRU_CHUNK
    rows = tc * nb
    n_chunks = seq // tc
    front = (CONV_WIDTH // 2) * nb

    xt_ref[pl.ds(0, front), :] = jnp.zeros((front, LRU_BLOCK), f32)
    xt_ref[pl.ds(front + seq * nb, nb), :] = jnp.zeros((nb, LRU_BLOCK), f32)
    for b in range(nb):
        xt_ref[pl.ds(front + b, seq, stride=nb), :] = xb_ref[b].astype(f32)

    lam = lam_ref[...]
    neg_c_sp = -LRU_C * jax.nn.softplus(-lam)
    cw = cw_ref[...]
    cb = cb_ref[...]

    def gates(d, r0):
        xc = cb
        for j in range(CONV_WIDTH):
            xc = xc + cw[j:j + 1, :] * xt_ref[pl.ds(pl.multiple_of(r0 + j * nb, nb), rows), :]
        g = jnp.dot(xc.astype(bf16), wg_ref[d], preferred_element_type=f32) + bg_ref[d]
        r = jax.nn.sigmoid(g[:, :LRU_BLOCK])
        i = jax.nn.sigmoid(g[:, LRU_BLOCK:])
        log_a = neg_c_sp[d:d + 1, :] * r
        a = jnp.exp(log_a)
        u = a * a
        z = 2.0 * log_a
        one_minus = jnp.where(u == 1.0, -z, (1.0 - u) * z / jnp.log(u))
        a_ref[...] = a
        b_ref[...] = jnp.sqrt(one_minus) * (i * xc)

    def fwd_chunk(c, h):
        r0 = pl.multiple_of(c * rows, rows)
        gates(0, r0)

        def step(t, h):
            rr = pl.multiple_of(t * nb, nb)
            h = a_ref[pl.ds(rr, nb), :] * h + b_ref[pl.ds(rr, nb), :]
            hs_ref[pl.ds(r0 + rr, nb), :] = h
            return h

        return lax.fori_loop(0, tc, step, h, unroll=8)

    lax.fori_loop(0, n_chunks, fwd_chunk, jnp.zeros((nb, LRU_BLOCK), f32))

    def bwd_chunk(ci, h):
        c = n_chunks - 1 - ci
        r0 = pl.multiple_of(c * rows, rows)
        gates(1, r0)

        def step(ti, h):
            rr = pl.multiple_of((tc - 1 - ti) * nb, nb)
            h = a_ref[pl.ds(rr, nb), :] * h + b_ref[pl.ds(rr, nb), :]
            hs_ref[pl.ds(r0 + rr, nb), :] = hs_ref[pl.ds(r0 + rr, nb), :] + h
            return h

        h = lax.fori_loop(0, tc, step, h, unroll=8)
        t0 = pl.multiple_of(c * tc, tc)
        for b in range(nb):
            hr = hs_ref[pl.ds(r0 + b, tc, stride=nb), :]
            gate = jax.nn.gelu(gb_ref[b, pl.ds(t0, tc), :].astype(f32))
            o_ref[b, pl.ds(t0, tc), :] = (hr * gate).astype(o_ref.dtype)
        return h

    lax.fori_loop(0, n_chunks, bwd_chunk, jnp.zeros((nb, LRU_BLOCK), f32))


def _lru_group(proj3, conv_w, conv_b, wg, bg, lam):
    nb, seq, _ = proj3.shape
    g = N_LRU_BLOCKS
    return pl.pallas_call(
        _lru_kernel,
        out_shape=jax.ShapeDtypeStruct((nb, seq, D_LRU), bf16),
        grid=(g,),
        in_specs=[
            pl.BlockSpec((nb, seq, LRU_BLOCK), lambda j: (0, 0, j)),
            pl.BlockSpec((nb, seq, LRU_BLOCK), lambda j: (0, 0, g + j)),
            pl.BlockSpec((CONV_WIDTH, LRU_BLOCK), lambda j: (0, j)),
            pl.BlockSpec((1, LRU_BLOCK), lambda j: (0, j)),
            pl.BlockSpec((2, None, LRU_BLOCK, 2 * LRU_BLOCK), lambda j: (0, j, 0, 0)),
            pl.BlockSpec((2, None, 1, 2 * LRU_BLOCK), lambda j: (0, j, 0, 0)),
            pl.BlockSpec((2, LRU_BLOCK), lambda j: (0, j)),
        ],
        out_specs=pl.BlockSpec((nb, seq, LRU_BLOCK), lambda j: (0, 0, j)),
        scratch_shapes=[
            pltpu.VMEM(((seq + CONV_WIDTH - 1) * nb, LRU_BLOCK), f32),
            pltpu.VMEM((seq * nb, LRU_BLOCK), f32),
            pltpu.VMEM((LRU_CHUNK * nb, LRU_BLOCK), f32),
            pltpu.VMEM((LRU_CHUNK * nb, LRU_BLOCK), f32),
        ],
        compiler_params=_cparams(("parallel",), VMEM_LIMIT_BIG),
        name="rg_lru",
    )(proj3, proj3, conv_w, conv_b.reshape(1, D_LRU), wg, bg, lam)


def _t5_bucket(rel):
    nbk = N_BUCKETS // 2
    ret = jnp.where(rel > 0, nbk, 0)
    n = jnp.abs(rel)
    nf = jnp.maximum(n, 1).astype(f32)
    large = MAX_EXACT + (jnp.log(nf / MAX_EXACT) / math.log(MAX_DISTANCE / MAX_EXACT)
                         * (nbk - MAX_EXACT)).astype(i32)
    large = jnp.minimum(large, nbk - 1)
    return ret + jnp.where(n < MAX_EXACT, n, large)


def _attn_bias_tables(rel_bias):
    qb = Q_BLOCK
    rb = rel_bias.astype(f32)
    i = jnp.arange(qb, dtype=i32)[:, None]

    def tile(step, dilation, valid):
        t = rb[_t5_bucket(step * dilation)]
        t = jnp.where(valid[..., None], t, NEG_INF)
        return jnp.moveaxis(t, -1, 0)

    j = jnp.arange(2 * qb, dtype=i32)[None, :]
    step1 = j - HALF - i
    band1 = jnp.abs(step1) <= HALF
    b1 = jnp.stack([tile(step1, 1, band1),
                    tile(step1, 1, band1 & (j >= HALF)),
                    tile(step1, 1, band1 & (j < 2 * qb - HALF))],
                   axis=1)
    lk = jnp.arange(qb, dtype=i32)[None, :]
    b2 = []
    for jq in range(4):
        step2 = jnp.concatenate([4 * (lk - i) + (jk - jq) for jk in range(4)], axis=1)
        b2.append(tile(step2, 4, jnp.abs(step2) <= HALF))
    b2 = jnp.stack(b2, axis=1)
    step3 = lk - i
    b3 = tile(step3, 16, jnp.abs(step3) <= HALF)
    return b1, b2, b3


def _softmax_parts(s):
    m = jnp.max(s, axis=-1, keepdims=True)
    p = jnp.exp(s - m)
    return m, p, jnp.sum(p, axis=-1, keepdims=True)


def _attn_kernel(q_ref, k_ref, v_ref, b1_ref, b2_ref, b3_ref, o_ref,
                 stage_ref, qp_ref, kp_ref, vp_ref, kpad_ref, vpad_ref, o23_ref, l23_ref):
    seq = q_ref.shape[0]
    qb = Q_BLOCK
    nblk = seq // qb
    per_res = seq // RES16
    dn = (((1,), (1,)), ((), ()))

    for src, dst in ((q_ref, qp_ref), (k_ref, kp_ref), (v_ref, vp_ref)):
        stage_ref[...] = src[...].astype(f32)
        for r in range(RES16):
            dst[pl.ds(r * per_res, per_res), :] = stage_ref[pl.ds(r, per_res, stride=RES16), :].astype(bf16)

    zpad = jnp.zeros((HALF, HEAD_DIM), bf16)
    kpad_ref[pl.ds(0, HALF), :] = zpad
    vpad_ref[pl.ds(0, HALF), :] = zpad
    kpad_ref[pl.ds(HALF + seq, HALF), :] = zpad
    vpad_ref[pl.ds(HALF + seq, HALF), :] = zpad
    kpad_ref[pl.ds(HALF, seq), :] = k_ref[...]
    vpad_ref[pl.ds(HALF, seq), :] = v_ref[...]

    def res_block(r, carry):
        jq = r // 4
        r4 = r - 4 * jq
        q = qp_ref[pl.ds(pl.multiple_of(r * per_res, per_res), per_res), :]
        ks = [kp_ref[pl.ds(pl.multiple_of((r4 + 4 * jk) * per_res, per_res), per_res), :] for jk in range(4)]
        vs = [vp_ref[pl.ds(pl.multiple_of((r4 + 4 * jk) * per_res, per_res), per_res), :] for jk in range(4)]
        k4 = jnp.concatenate(ks, axis=0)
        v4 = jnp.concatenate(vs, axis=0)
        s2 = lax.dot_general(q, k4, dn, preferred_element_type=f32) + b2_ref[jq]
        m2, p2, d2 = _softmax_parts(s2)
        o2 = jnp.dot(p2.astype(bf16), v4, preferred_element_type=f32)
        k1 = kp_ref[pl.ds(pl.multiple_of(r * per_res, per_res), per_res), :]
        v1 = vp_ref[pl.ds(pl.multiple_of(r * per_res, per_res), per_res), :]
        s3 = lax.dot_general(q, k1, dn, preferred_element_type=f32) + b3_ref[...]
        m3, p3, d3 = _softmax_parts(s3)
        o3 = jnp.dot(p3.astype(bf16), v1, preferred_element_type=f32)
        m23 = jnp.maximum(m2, m3)
        w2 = jnp.exp(m2 - m23)
        w3 = jnp.exp(m3 - m23)
        d23 = w2 * d2 + w3 * d3
        o23 = (w2 * o2 + w3 * o3) / d23
        l23 = m23 + jnp.log(d23)
        row0 = pl.multiple_of(r * per_res, per_res)
        o23_ref[pl.ds(row0, per_res), :] = o23
        l23_ref[pl.ds(row0, per_res), :] = jnp.broadcast_to(l23, (per_res, HEAD_DIM))
        return carry

    lax.fori_loop(0, RES16, res_block, 0)

    for r in range(RES16):
        stage_ref[pl.ds(r, per_res, stride=RES16), :] = o23_ref[pl.ds(r * per_res, per_res), :]
    for r in range(RES16):
        o23_ref[pl.ds(r, per_res, stride=RES16), :] = l23_ref[pl.ds(r * per_res, per_res), :]

    def nat_block(n, carry):
        row0 = pl.multiple_of(n * qb, qb)
        q = q_ref[pl.ds(row0, qb), :]
        kw = kpad_ref[pl.ds(row0, 2 * qb), :]
        vw = vpad_ref[pl.ds(row0, 2 * qb), :]
        which = jnp.where(n == 0, 1, jnp.where(n == nblk - 1, 2, 0))
        s1 = lax.dot_general(q, kw, dn, preferred_element_type=f32) + b1_ref[which]
        m1, p1, d1 = _softmax_parts(s1)
        o1 = jnp.dot(p1.astype(bf16), vw, preferred_element_type=f32)
        l23 = o23_ref[pl.ds(row0, qb), :]
        on23 = stage_ref[pl.ds(row0, qb), :]
        mm = jnp.maximum(m1, l23)
        w1 = jnp.exp(m1 - mm)
        w23 = jnp.exp(l23 - mm)
        o_ref[pl.ds(row0, qb), :] = ((w1 * o1 + w23 * on23) / (w1 * d1 + w23)).astype(o_ref.dtype)
        return carry

    lax.fori_loop(0, nblk, nat_block, 0)


def _dilated_attention(proj3, b1, b2, b3):
    nb, seq, _ = proj3.shape
    assert seq // RES16 == Q_BLOCK
    q0 = 2 * D_LRU // HEAD_DIM
    k0 = q0 + N_HEADS
    v0 = k0 + N_HEADS
    return pl.pallas_call(
        _attn_kernel,
        out_shape=jax.ShapeDtypeStruct((nb, seq, D_ATTN), bf16),
        grid=(N_HEADS, nb),
        in_specs=[
            pl.BlockSpec((None, seq, HEAD_DIM), lambda h, b: (b, 0, q0 + h)),
            pl.BlockSpec((None, seq, HEAD_DIM), lambda h, b: (b, 0, k0 + h)),
            pl.BlockSpec((None, seq, HEAD_DIM), lambda h, b: (b, 0, v0 + h)),
            pl.BlockSpec((None, 3, Q_BLOCK, 2 * Q_BLOCK), lambda h, b: (h, 0, 0, 0)),
            pl.BlockSpec((None, 4, Q_BLOCK, 4 * Q_BLOCK), lambda h, b: (h, 0, 0, 0)),
            pl.BlockSpec((None, Q_BLOCK, Q_BLOCK), lambda h, b: (h, 0, 0)),
        ],
        out_specs=pl.BlockSpec((None, seq, HEAD_DIM), lambda h, b: (b, 0, h)),
        scratch_shapes=[
            pltpu.VMEM((seq, HEAD_DIM), f32),
            pltpu.VMEM((seq, HEAD_DIM), bf16),
            pltpu.VMEM((seq, HEAD_DIM), bf16),
            pltpu.VMEM((seq, HEAD_DIM), bf16),
            pltpu.VMEM((seq + 2 * HALF, HEAD_DIM), bf16),
            pltpu.VMEM((seq + 2 * HALF, HEAD_DIM), bf16),
            pltpu.VMEM((seq, HEAD_DIM), f32),
            pltpu.VMEM((seq, HEAD_DIM), f32),
        ],
        compiler_params=_cparams(("parallel", "parallel"), VMEM_LIMIT_MID),
        name="dilated_attn",
    )(proj3, proj3, proj3, b1, b2, b3)


def _layer_norm(z, g, b):
    mu = jnp.mean(z, axis=-1, keepdims=True)
    zc = z - mu
    var = jnp.mean(zc * zc, axis=-1, keepdims=True)
    return zc * lax.rsqrt(var + LN_EPS) * g + b


def _rms_norm(v, g):
    return v * lax.rsqrt(jnp.mean(v * v, axis=-1, keepdims=True) + LN_EPS) * g


def _pack_bf16_pairs(v):
    n = v.shape[1] // 2
    lo = pltpu.bitcast(v[:, :n].astype(bf16).astype(f32), u32)
    hi = pltpu.bitcast(v[:, n:].astype(bf16).astype(f32), u32)
    return (lo >> 16) | (hi & jnp.uint32(0xFFFF0000))


def _unpack_bf16_pairs(p):
    lo = pltpu.bitcast(p << 16, f32)
    hi = pltpu.bitcast(p & jnp.uint32(0xFFFF0000), f32)
    return jnp.concatenate([lo.astype(bf16), hi.astype(bf16)], axis=1)


def _outproj_kernel(lru_ref, att_ref, x_ref, gl_ref, ga_ref, w_ref, gm_ref, shf_ref, scf_ref,
                    lng_ref, lnb_ref, wr_ref, br_ref,
                    x1_ref, hp_ref, te_ref, tw_ref, rk_ref, cnt_ref, carry_ref):
    tm = x_ref.shape[0]
    step = pl.program_id(0)

    @pl.when(step == 0)
    def _():
        carry_ref[...] = jnp.zeros_like(carry_ref)

    ln = _rms_norm(lru_ref[...].astype(f32), gl_ref[...]).astype(bf16)
    an = _rms_norm(att_ref[...].astype(f32), ga_ref[...]).astype(bf16)
    y = (jnp.dot(ln, w_ref[pl.ds(0, D_LRU), :], preferred_element_type=f32)
         + jnp.dot(an, w_ref[pl.ds(D_LRU, D_ATTN), :], preferred_element_type=f32))
    x1 = _layer_norm(DEEPNORM_ALPHA * x_ref[...] + gm_ref[...] * y, lng_ref[...], lnb_ref[...])
    x1_ref[...] = x1
    hf = x1 * (1.0 + scf_ref[...]) + shf_ref[...]
    hp_ref[...] = _pack_bf16_pairs(hf)

    logits = jnp.dot(hf, wr_ref[...], precision=lax.Precision.HIGHEST,
                     preferred_element_type=f32) + br_ref[...]
    lane = lax.broadcasted_iota(i32, (tm, N_EXPERTS), 1)
    col4 = lax.broadcasted_iota(i32, (tm, TOP_K), 1)
    work = logits
    vals, idxs = [], []
    for _k in range(TOP_K):
        m = jnp.max(work, axis=-1, keepdims=True)
        idx = jnp.min(jnp.where(work == m, lane, N_EXPERTS), axis=-1, keepdims=True)
        vals.append(m)
        idxs.append(idx)
        work = jnp.where(lane == idx, -jnp.inf, work)
    exps = [jnp.exp(v - vals[0]) for v in vals]
    den = exps[0] + exps[1] + exps[2] + exps[3]

    onehot = jnp.zeros((tm, N_EXPERTS), f32)
    for idx in idxs:
        onehot = onehot + (lane == idx).astype(f32)
    ri = lax.broadcasted_iota(i32, (tm, tm), 0)
    ci = lax.broadcasted_iota(i32, (tm, tm), 1)
    lower = (ri > ci).astype(bf16)
    before = jnp.dot(lower, onehot.astype(bf16), preferred_element_type=f32) + carry_ref[...]
    te = jnp.zeros((tm, TOP_K), i32)
    tw = jnp.zeros((tm, TOP_K), f32)
    rk = jnp.zeros((tm, TOP_K), i32)
    for k in range(TOP_K):
        rank_k = jnp.sum(jnp.where(lane == idxs[k], before, 0.0), axis=-1, keepdims=True)
        te = jnp.where(col4 == k, idxs[k], te)
        tw = jnp.where(col4 == k, exps[k] / den, tw)
        rk = jnp.where(col4 == k, rank_k.astype(i32), rk)
    te_ref[...] = te
    tw_ref[...] = tw
    rk_ref[...] = rk
    carry_ref[...] = carry_ref[...] + jnp.sum(onehot, axis=0, keepdims=True)
    cnt_ref[...] = carry_ref[...].astype(i32)


def _out_proj_router(lru2, att2, x2, g_lru, g_attn, w_out_bf, mod3, ln_g, ln_b, w_router, b_router, seq):
    t, d = x2.shape
    tm = TOK_TILE
    tps = seq // tm
    row = lambda i: (i, 0)
    const = lambda i: (0, 0)
    return pl.pallas_call(
        _outproj_kernel,
        out_shape=(
            jax.ShapeDtypeStruct((t, d), f32),
            jax.ShapeDtypeStruct((t, d // 2), u32),
            jax.ShapeDtypeStruct((t, TOP_K), i32),
            jax.ShapeDtypeStruct((t, TOP_K), f32),
            jax.ShapeDtypeStruct((t, TOP_K), i32),
            jax.ShapeDtypeStruct((1, N_EXPERTS), i32),
        ),
        grid=(t // tm,),
        in_specs=[
            pl.BlockSpec((tm, D_LRU), row),
            pl.BlockSpec((tm, D_ATTN), row),
            pl.BlockSpec((tm, d), row),
            pl.BlockSpec((1, D_LRU), const),
            pl.BlockSpec((1, D_ATTN), const),
            pl.BlockSpec((d, d), const),
            pl.BlockSpec((None, 1, d), lambda i: (i // tps, 0, 2)),
            pl.BlockSpec((None, 1, d), lambda i: (i // tps, 0, 3)),
            pl.BlockSpec((None, 1, d), lambda i: (i // tps, 0, 4)),
            pl.BlockSpec((1, d), const),
            pl.BlockSpec((1, d), const),
            pl.BlockSpec((d, N_EXPERTS), const),
            pl.BlockSpec((1, N_EXPERTS), const),
        ],
        out_specs=(
            pl.BlockSpec((tm, d), row),
            pl.BlockSpec((tm, d // 2), row),
            pl.BlockSpec((tm, TOP_K), row),
            pl.BlockSpec((tm, TOP_K), row),
            pl.BlockSpec((tm, TOP_K), row),
            pl.BlockSpec((1, N_EXPERTS), const),
        ),
        scratch_shapes=[pltpu.VMEM((1, N_EXPERTS), f32)],
        compiler_params=_cparams(("arbitrary",), VMEM_LIMIT_BIG),
        name="out_proj_router",
    )(lru2, att2, x2, g_lru.reshape(1, -1), g_attn.reshape(1, -1), w_out_bf, mod3, mod3, mod3,
      ln_g.reshape(1, -1), ln_b.reshape(1, -1), w_router, b_router.reshape(1, -1))


def _dispatch_kernel(dest_ref, pad_start_ref, pad_len_ref, nt_ref, hp_ref, xs_ref, zero_ref, sem):
    tm = hp_ref.shape[0]
    n_asg = tm * TOP_K
    n_tiles = xs_ref.shape[0] // EXP_TILE

    def copy(a):
        return pltpu.make_async_copy(hp_ref.at[pl.ds(a // TOP_K, 1), :],
                                     xs_ref.at[pl.ds(dest_ref[a], 1), :], sem)

    def start(a, c):
        copy(a).start()
        return c

    def wait(a, c):
        copy(a).wait()
        return c

    lax.fori_loop(0, n_asg, start, 0, unroll=8)
    lax.fori_loop(0, n_asg, wait, 0, unroll=8)

    @pl.when(pl.program_id(0) == pl.num_programs(0) - 1)
    def _():
        zero_ref[...] = jnp.zeros_like(zero_ref)

        def per_expert(e, c):
            p0 = pad_start_ref[e]
            n = pad_len_ref[e]

            def zcopy(r):
                return pltpu.make_async_copy(zero_ref.at[pl.ds(0, 1), :], xs_ref.at[pl.ds(p0 + r, 1), :], sem)

            lax.fori_loop(0, n, lambda r, c2: (zcopy(r).start(), c2)[1], 0)
            lax.fori_loop(0, n, lambda r, c2: (zcopy(r).wait(), c2)[1], 0)
            return c

        lax.fori_loop(0, N_EXPERTS, per_expert, 0)

        def tail_tile(m, c):
            cp = pltpu.make_async_copy(zero_ref, xs_ref.at[pl.ds(pl.multiple_of(m * EXP_TILE, EXP_TILE), EXP_TILE), :], sem)
            cp.start()
            cp.wait()
            return c

        lax.fori_loop(nt_ref[0], n_tiles, tail_tile, 0)


def _dispatch(hp, dest_flat, pad_start, pad_len, n_tiles_used, n_rows):
    t, half = hp.shape
    tm = TOK_TILE
    smem = pl.BlockSpec(memory_space=pltpu.SMEM)
    return pl.pallas_call(
        _dispatch_kernel,
        out_shape=jax.ShapeDtypeStruct((n_rows, half), u32),
        grid=(t // tm,),
        in_specs=[
            pl.BlockSpec((tm * TOP_K,), lambda i: (i,), memory_space=pltpu.SMEM),
            smem, smem, smem,
            pl.BlockSpec((tm, half), lambda i: (i, 0)),
        ],
        out_specs=pl.BlockSpec(memory_space=pl.ANY),
        scratch_shapes=[pltpu.VMEM((EXP_TILE, half), u32), pltpu.SemaphoreType.DMA(())],
        compiler_params=_cparams(("arbitrary",), VMEM_LIMIT_MID),
        name="moe_dispatch",
    )(dest_flat, pad_start, pad_len, n_tiles_used, hp)


def _gate_up_kernel(te_ref, nt_ref, xs_ref, wg_ref, wu_ref, bg_ref, bu_ref, h_ref, w_scr):
    m = pl.program_id(1)
    fc = wg_ref.shape[1]
    changed = jnp.logical_or(m == 0, te_ref[m] != te_ref[jnp.maximum(m - 1, 0)])

    @pl.when(jnp.logical_and(changed, m < nt_ref[0]))
    def _():
        w_scr[:, pl.ds(0, fc)] = wg_ref[...].astype(bf16)
        w_scr[:, pl.ds(fc, fc)] = wu_ref[...].astype(bf16)

    @pl.when(m < nt_ref[0])
    def _():
        x = _unpack_bf16_pairs(xs_ref[...])
        gu = jnp.dot(x, w_scr[...], preferred_element_type=f32)
        gate = jnp.minimum(gu[:, :fc] + bg_ref[...], SWIGLU_LIMIT)
        up = jnp.clip(gu[:, fc:] + bu_ref[...], -SWIGLU_LIMIT, SWIGLU_LIMIT)
        glu = gate * jax.nn.sigmoid(gate * SWIGLU_ALPHA)
        h_ref[...] = _pack_bf16_pairs((up + 1.0) * glu)

    @pl.when(m >= nt_ref[0])
    def _():
        h_ref[...] = jnp.zeros_like(h_ref)


def _gate_up(xs, w_gate_up, b_gate_up3, tile_e, n_tiles_used):
    p, half = xs.shape
    d = 2 * half
    fc = D_FF // 2
    n_pass = D_FF // fc
    tm = EXP_TILE
    grid_spec = pltpu.PrefetchScalarGridSpec(
        num_scalar_prefetch=2,
        grid=(n_pass, p // tm),
        in_specs=[
            pl.BlockSpec((tm, half), lambda j, m, te, nt: (m, 0)),
            pl.BlockSpec((None, d, fc), lambda j, m, te, nt: (te[m], 0, j)),
            pl.BlockSpec((None, d, fc), lambda j, m, te, nt: (te[m], 0, n_pass + j)),
            pl.BlockSpec((None, 1, fc), lambda j, m, te, nt: (te[m], 0, j)),
            pl.BlockSpec((None, 1, fc), lambda j, m, te, nt: (te[m], 0, n_pass + j)),
        ],
        out_specs=pl.BlockSpec((tm, fc // 2), lambda j, m, te, nt: (m, j)),
        scratch_shapes=[pltpu.VMEM((d, 2 * fc), bf16)],
    )
    return pl.pallas_call(
        _gate_up_kernel,
        out_shape=jax.ShapeDtypeStruct((p, D_FF // 2), u32),
        grid_spec=grid_spec,
        compiler_params=_cparams(("arbitrary", "arbitrary"), VMEM_LIMIT_BIG),
        name="moe_gate_up",
    )(tile_e, n_tiles_used, xs, w_gate_up, w_gate_up, b_gate_up3, b_gate_up3)


def _down_kernel(te_ref, nt_ref, h_ref, w_ref, b_ref, y_ref, w_scr):
    m = pl.program_id(0)
    fc = D_FF // 2
    changed = jnp.logical_or(m == 0, te_ref[m] != te_ref[jnp.maximum(m - 1, 0)])

    @pl.when(jnp.logical_and(changed, m < nt_ref[0]))
    def _():
        w_scr[...] = w_ref[...].astype(bf16)

    @pl.when(m < nt_ref[0])
    def _():
        hp = h_ref[...]
        h = jnp.concatenate([_unpack_bf16_pairs(hp[:, :fc // 2]), _unpack_bf16_pairs(hp[:, fc // 2:])], axis=1)
        y = jnp.dot(h, w_scr[...], preferred_element_type=f32) + b_ref[...]
        y_ref[...] = _pack_bf16_pairs(y)

    @pl.when(m >= nt_ref[0])
    def _():
        y_ref[...] = jnp.zeros_like(y_ref)


def _down(hs, w_down, b_down3, tile_e, n_tiles_used):
    p, half = hs.shape
    d_ff = 2 * half
    d = w_down.shape[2]
    tm = EXP_TILE
    grid_spec = pltpu.PrefetchScalarGridSpec(
        num_scalar_prefetch=2,
        grid=(p // tm,),
        in_specs=[
            pl.BlockSpec((tm, half), lambda m, te, nt: (m, 0)),
            pl.BlockSpec((None, d_ff, d), lambda m, te, nt: (te[m], 0, 0)),
            pl.BlockSpec((None, 1, d), lambda m, te, nt: (te[m], 0, 0)),
        ],
        out_specs=pl.BlockSpec((tm, d // 2), lambda m, te, nt: (m, 0)),
        scratch_shapes=[pltpu.VMEM((d_ff, d), bf16)],
    )
    return pl.pallas_call(
        _down_kernel,
        out_shape=jax.ShapeDtypeStruct((p, d // 2), u32),
        grid_spec=grid_spec,
        compiler_params=_cparams(("arbitrary",), VMEM_LIMIT_BIG),
        name="moe_down",
    )(tile_e, n_tiles_used, hs, w_down, b_down3)


def _combine_kernel(dest_ref, ys_ref, tw_ref, x1_ref, gf_ref, lng_ref, lnb_ref, o_ref, buf_ref, sem):
    tm = x1_ref.shape[0]
    n_asg = tm * TOP_K

    def copy(a):
        tok = a // TOP_K
        k = a - tok * TOP_K
        return pltpu.make_async_copy(ys_ref.at[pl.ds(dest_ref[a], 1), :],
                                     buf_ref.at[k, pl.ds(tok, 1), :], sem)

    lax.fori_loop(0, n_asg, lambda a, c: (copy(a).start(), c)[1], 0, unroll=8)
    lax.fori_loop(0, n_asg, lambda a, c: (copy(a).wait(), c)[1], 0, unroll=8)

    tw = tw_ref[...]
    y = jnp.zeros((tm, D_MODEL), f32)
    for k in range(TOP_K):
        y = y + tw[:, k:k + 1] * _unpack_bf16_pairs(buf_ref[k]).astype(f32)
    o_ref[...] = _layer_norm(DEEPNORM_ALPHA * x1_ref[...] + gf_ref[...] * y, lng_ref[...], lnb_ref[...])


def _combine(ys, dest_flat, top_w, x1, mod3, ln_g, ln_b, seq):
    t, d = x1.shape
    tm = TOK_TILE
    tps = seq // tm
    grid_spec = pltpu.PrefetchScalarGridSpec(
        num_scalar_prefetch=0,
        grid=(t // tm,),
        in_specs=[
            pl.BlockSpec((tm * TOP_K,), lambda i: (i,), memory_space=pltpu.SMEM),
            pl.BlockSpec(memory_space=pl.ANY),
            pl.BlockSpec((tm, TOP_K), lambda i: (i, 0)),
            pl.BlockSpec((tm, d), lambda i: (i, 0)),
            pl.BlockSpec((None, 1, d), lambda i: (i // tps, 0, 5)),
            pl.BlockSpec((1, d), lambda i: (0, 0)),
            pl.BlockSpec((1, d), lambda i: (0, 0)),
        ],
        out_specs=pl.BlockSpec((tm, d), lambda i: (i, 0)),
        scratch_shapes=[pltpu.VMEM((TOP_K, tm, d // 2), u32), pltpu.SemaphoreType.DMA(())],
    )
    return pl.pallas_call(
        _combine_kernel,
        out_shape=jax.ShapeDtypeStruct((t, d), f32),
        grid_spec=grid_spec,
        compiler_params=_cparams(("arbitrary",), VMEM_LIMIT_MID),
        name="moe_combine",
    )(dest_flat, ys, top_w, x1, mod3, ln_g.reshape(1, -1), ln_b.reshape(1, -1))


def _routing_tables(top_e, rank, counts, n_tiles):
    tm = EXP_TILE
    counts = counts.reshape(-1)
    padded = (counts + tm - 1) // tm * tm
    pend = jnp.cumsum(padded)
    pstart = pend - padded
    dest = (pstart[top_e] + rank).reshape(-1).astype(i32)
    tile_e = jnp.searchsorted(pend, jnp.arange(n_tiles, dtype=i32) * tm, side="right")
    tile_e = jnp.minimum(tile_e, N_EXPERTS - 1).astype(i32)
    n_used = (pend[-1] // tm).astype(i32).reshape(1)
    return dest, tile_e, n_used, (pstart + counts).astype(i32), (padded - counts).astype(i32)


def kernel(x, c, rel_bias, w_ada, b_ada, w_in, conv_w, conv_b, lru_w_a, lru_b_a, lru_w_x, lru_b_x, lru_lambda, norm_lru_g, norm_attn_g, w_out, ln_mix_g, ln_mix_b, w_router, b_router, w_gate_up, b_gate_up, w_down, b_down, ln_ffn_g, ln_ffn_b):
    nb, seq, d = x.shape
    t = nb * seq
    mod = _ada_mod(c, w_ada, b_ada)
    b1, b2, b3 = _attn_bias_tables(rel_bias)
    n_tiles = t * TOP_K // EXP_TILE + N_EXPERTS
    q_lo = 2 * D_LRU
    x2 = x.reshape(t, d)
    for l in range(DEPTH):
        mod3 = mod[l].reshape(nb, 1, 6 * d)
        w_in_l = w_in[l].at[:, q_lo:q_lo + D_ATTN].multiply(HEAD_DIM ** -0.5).astype(bf16)
        proj = _in_proj(x2, mod3, w_in_l, seq).reshape(nb, seq, D_IN)
        wg = jnp.concatenate([lru_w_a[l], lru_w_x[l]], axis=-1).astype(bf16)
        bg = jnp.concatenate([lru_b_a[l].reshape(2, N_LRU_BLOCKS, 1, LRU_BLOCK),
                              lru_b_x[l].reshape(2, N_LRU_BLOCKS, 1, LRU_BLOCK)], axis=-1)
        lru = _lru_group(proj, conv_w[l], conv_b[l], wg, bg, lru_lambda[l])
        att = _dilated_attention(proj, b1, b2, b3)
        x1, hp, top_e, top_w, rank, counts = _out_proj_router(
            lru.reshape(t, D_LRU), att.reshape(t, D_ATTN), x2, norm_lru_g[l], norm_attn_g[l],
            w_out[l].astype(bf16), mod3, ln_mix_g[l], ln_mix_b[l], w_router[l], b_router[l], seq)
        dest, tile_e, n_used, pad_start, pad_len = _routing_tables(top_e, rank, counts, n_tiles)
        xs = _dispatch(hp, dest, pad_start, pad_len, n_used, n_tiles * EXP_TILE)
        hs = _gate_up(xs, w_gate_up[l], b_gate_up[l].reshape(N_EXPERTS, 1, 2 * D_FF), tile_e, n_used)
        ys = _down(hs, w_down[l], b_down[l].reshape(N_EXPERTS, 1, d), tile_e, n_used)
        x2 = _combine(ys, dest, top_w, x1, mod3, ln_ffn_g[l], ln_ffn_b[l], seq)
    return x2.reshape(nb, seq, d)
```

```python
import functools
import math

import jax
import jax.numpy as jnp
from jax import lax
from jax.experimental import pallas as pl
from jax.experimental.pallas import tpu as pltpu

f32 = jnp.float32
bf16 = jnp.bfloat16
i32 = jnp.int32
u32 = jnp.uint32

D_MODEL = 2048
DEPTH = 2
D_LRU = 1024
N_LRU_BLOCKS = 8
LRU_BLOCK = 128
CONV_WIDTH = 4
LRU_C = 8.0
D_ATTN = 1024
HEAD_DIM = 128
N_HEADS = 8
D_IN = 2 * D_LRU + 3 * D_ATTN
NEG_INF = -1e30
N_BUCKETS = 32
MAX_EXACT = 8
MAX_DISTANCE = 1024
N_EXPERTS = 32
TOP_K = 4
D_FF = D_MODEL
SWIGLU_LIMIT = 7.0
SWIGLU_ALPHA = 1.702
DEEPNORM_ALPHA = (2 * DEPTH) ** 0.25
LN_EPS = 1e-5

VMEM_LIMIT_BIG = 56 * 1024 * 1024
VMEM_LIMIT_MID = 40 * 1024 * 1024

ROW_TILE = 512
TOK_TILE = 256
EXP_TILE = 512
LRU_CHUNK = 128
Q_BLOCK = 128
NAT_BLOCK = 256
HALF = 64
RES16 = 16


def _cparams(sem, vmem=None):
    return pltpu.CompilerParams(dimension_semantics=sem, vmem_limit_bytes=vmem)


def _ada_kernel(c_ref, w_ref, b_ref, o_ref):
    c = c_ref[...]
    ca = c * jax.nn.sigmoid(c)
    o_ref[0] = jnp.dot(ca, w_ref[0], precision=lax.Precision.HIGHEST,
                       preferred_element_type=f32) + b_ref[0]


def _ada_mod(c, w_ada, b_ada):
    depth, d, n = w_ada.shape
    b = c.shape[0]
    tn = 1024
    return pl.pallas_call(
        _ada_kernel,
        out_shape=jax.ShapeDtypeStruct((depth, b, n), f32),
        grid=(depth, n // tn),
        in_specs=[
            pl.BlockSpec((b, d), lambda l, j: (0, 0)),
            pl.BlockSpec((1, d, tn), lambda l, j: (l, 0, j)),
            pl.BlockSpec((1, 1, tn), lambda l, j: (l, 0, j)),
        ],
        out_specs=pl.BlockSpec((1, b, tn), lambda l, j: (l, 0, j)),
        compiler_params=_cparams(("parallel", "parallel"), VMEM_LIMIT_MID),
        name="ada_mod",
    )(c, w_ada, b_ada.reshape(depth, 1, n))


def _inproj_kernel(x_ref, sh_ref, sc_ref, w_ref, o_ref):
    h = x_ref[...] * (1.0 + sc_ref[...]) + sh_ref[...]
    o_ref[...] = jnp.dot(h.astype(bf16), w_ref[...], preferred_element_type=f32).astype(o_ref.dtype)


def _in_proj(x2, mod3, w_in_bf, seq):
    t, d = x2.shape
    n = w_in_bf.shape[1]
    tn = n // 2
    tiles_per_seq = seq // ROW_TILE
    return pl.pallas_call(
        _inproj_kernel,
        out_shape=jax.ShapeDtypeStruct((t, n), bf16),
        grid=(n // tn, t // ROW_TILE),
        in_specs=[
            pl.BlockSpec((ROW_TILE, d), lambda j, i: (i, 0)),
            pl.BlockSpec((None, 1, d), lambda j, i: (i // tiles_per_seq, 0, 0)),
            pl.BlockSpec((None, 1, d), lambda j, i: (i // tiles_per_seq, 0, 1)),
            pl.BlockSpec((d, tn), lambda j, i: (0, j)),
        ],
        out_specs=pl.BlockSpec((ROW_TILE, tn), lambda j, i: (i, j)),
        compiler_params=_cparams(("parallel", "parallel"), VMEM_LIMIT_BIG),
        name="in_proj",
    )(x2, mod3, mod3, w_in_bf)


def _lru_kernel(xb_ref, gb_ref, cw_ref, cb_ref, wg_ref, bg_ref, lam_ref, o_ref,
                xt_ref, hs_ref, a_ref, b_ref):
    nb, seq, _ = xb_ref.shape
    tc = LRU_CHUNK
    rows = tc * nb
    n_chunks = seq // tc
    front = (CONV_WIDTH // 2) * nb

    xt_ref[pl.ds(0, front), :] = jnp.zeros((front, LRU_BLOCK), f32)
    xt_ref[pl.ds(front + seq * nb, nb), :] = jnp.zeros((nb, LRU_BLOCK), f32)
    for b in range(nb):
        xt_ref[pl.ds(front + b, seq, stride=nb), :] = xb_ref[b].astype(f32)

    lam = lam_ref[...]
    neg_c_sp = -LRU_C * jax.nn.softplus(-lam)
    cw = cw_ref[...]
    cb = cb_ref[...]

    def gates(d, r0):
        xc = cb
        for j in range(CONV_WIDTH):
            xc = xc + cw[j:j + 1, :] * xt_ref[pl.ds(pl.multiple_of(r0 + j * nb, nb), rows), :]
        g = jnp.dot(xc.astype(bf16), wg_ref[d], preferred_element_type=f32) + bg_ref[d]
        r = jax.nn.sigmoid(g[:, :LRU_BLOCK])
        i = jax.nn.sigmoid(g[:, LRU_BLOCK:])
        log_a = neg_c_sp[d:d + 1, :] * r
        a = jnp.exp(log_a)
        u = a * a
        z = 2.0 * log_a
        one_minus = jnp.where(u == 1.0, -z, (1.0 - u) * z / jnp.log(u))
        a_ref[...] = a
        b_ref[...] = jnp.sqrt(one_minus) * (i * xc)

    def fwd_chunk(c, h):
        r0 = pl.multiple_of(c * rows, rows)
        gates(0, r0)

        def step(t, h):
            rr = pl.multiple_of(t * nb, nb)
            h = a_ref[pl.ds(rr, nb), :] * h + b_ref[pl.ds(rr, nb), :]
            hs_ref[pl.ds(r0 + rr, nb), :] = h
            return h

        return lax.fori_loop(0, tc, step, h, unroll=8)

    lax.fori_loop(0, n_chunks, fwd_chunk, jnp.zeros((nb, LRU_BLOCK), f32))

    def bwd_chunk(ci, h):
        c = n_chunks - 1 - ci
        r0 = pl.multiple_of(c * rows, rows)
        gates(1, r0)

        def step(ti, h):
            rr = pl.multiple_of((tc - 1 - ti) * nb, nb)
            h = a_ref[pl.ds(rr, nb), :] * h + b_ref[pl.ds(rr, nb), :]
            hs_ref[pl.ds(r0 + rr, nb), :] = hs_ref[pl.ds(r0 + rr, nb), :] + h
            return h

        h = lax.fori_loop(0, tc, step, h, unroll=8)
        t0 = pl.multiple_of(c * tc, tc)
        for b in range(nb):
            hr = hs_ref[pl.ds(r0 + b, tc, stride=nb), :]
            gate = jax.nn.gelu(gb_ref[b, pl.ds(t0, tc), :].astype(f32))
            o_ref[b, pl.ds(t0, tc), :] = (hr * gate).astype(o_ref.dtype)
        return h

    lax.fori_loop(0, n_chunks, bwd_chunk, jnp.zeros((nb, LRU_BLOCK), f32))


def _lru_group(proj3, conv_w, conv_b, wg, bg, lam):
    nb, seq, _ = proj3.shape
    g = N_LRU_BLOCKS
    return pl.pallas_call(
        _lru_kernel,
        out_shape=jax.ShapeDtypeStruct((nb, seq, D_LRU), bf16),
        grid=(g,),
        in_specs=[
            pl.BlockSpec((nb, seq, LRU_BLOCK), lambda j: (0, 0, j)),
            pl.BlockSpec((nb, seq, LRU_BLOCK), lambda j: (0, 0, g + j)),
            pl.BlockSpec((CONV_WIDTH, LRU_BLOCK), lambda j: (0, j)),
            pl.BlockSpec((1, LRU_BLOCK), lambda j: (0, j)),
            pl.BlockSpec((2, None, LRU_BLOCK, 2 * LRU_BLOCK), lambda j: (0, j, 0, 0)),
            pl.BlockSpec((2, None, 1, 2 * LRU_BLOCK), lambda j: (0, j, 0, 0)),
            pl.BlockSpec((2, LRU_BLOCK), lambda j: (0, j)),
        ],
        out_specs=pl.BlockSpec((nb, seq, LRU_BLOCK), lambda j: (0, 0, j)),
        scratch_shapes=[
            pltpu.VMEM(((seq + CONV_WIDTH - 1) * nb, LRU_BLOCK), f32),
            pltpu.VMEM((seq * nb, LRU_BLOCK), f32),
            pltpu.VMEM((LRU_CHUNK * nb, LRU_BLOCK), f32),
            pltpu.VMEM((LRU_CHUNK * nb, LRU_BLOCK), f32),
        ],
        compiler_params=_cparams(("parallel",), VMEM_LIMIT_BIG),
        name="rg_lru",
    )(proj3, proj3, conv_w, conv_b.reshape(1, D_LRU), wg, bg, lam)


def _t5_bucket(rel):
    nbk = N_BUCKETS // 2
    ret = jnp.where(rel > 0, nbk, 0)
    n = jnp.abs(rel)
    nf = jnp.maximum(n, 1).astype(f32)
    large = MAX_EXACT + (jnp.log(nf / MAX_EXACT) / math.log(MAX_DISTANCE / MAX_EXACT)
                         * (nbk - MAX_EXACT)).astype(i32)
    large = jnp.minimum(large, nbk - 1)
    return ret + jnp.where(n < MAX_EXACT, n, large)


def _attn_bias_tables(rel_bias):
    qb = Q_BLOCK
    rb = rel_bias.astype(f32)
    i = jnp.arange(qb, dtype=i32)[:, None]

    def tile(step, dilation, valid):
        onehot = jax.nn.one_hot(_t5_bucket(step * dilation), N_BUCKETS, dtype=f32)
        t = jnp.einsum("qkn,nh->hqk", onehot, rb, precision=lax.Precision.HIGHEST)
        return jnp.where(valid[None], t, NEG_INF)

    i1 = jnp.arange(NAT_BLOCK, dtype=i32)[:, None]
    win = NAT_BLOCK + 2 * HALF
    j = jnp.arange(win, dtype=i32)[None, :]
    step1 = j - HALF - i1
    band1 = jnp.abs(step1) <= HALF
    b1 = jnp.stack([tile(step1, 1, band1),
                    tile(step1, 1, band1 & (j >= HALF)),
                    tile(step1, 1, band1 & (j < win - HALF))],
                   axis=1)
    lk = jnp.arange(qb, dtype=i32)[None, :]
    step2 = jnp.concatenate(
        [jnp.concatenate([4 * (lk - i) + (jk - jq) for jk in range(4)], axis=1) for jq in range(4)], axis=0)
    b2 = tile(step2, 4, jnp.abs(step2) <= HALF)
    step3 = lk - i
    b3 = tile(step3, 16, jnp.abs(step3) <= HALF)
    return b1, b2, b3


def _softmax_parts(s):
    m = jnp.max(s, axis=-1, keepdims=True)
    p = jnp.exp(s - m)
    return m, p, jnp.sum(p, axis=-1, keepdims=True)


def _attn_kernel(q_ref, k_ref, v_ref, b1_ref, b2_ref, b3_ref, o_ref,
                 stage_ref, qp_ref, kp_ref, vp_ref, kpad_ref, vpad_ref, o23_ref, l23_ref):
    seq = q_ref.shape[0]
    qb = NAT_BLOCK
    nblk = seq // qb
    per_res = seq // RES16
    cls = 4 * per_res
    dn = (((1,), (1,)), ((), ()))

    def perm_row0(r16):
        return ((r16 % 4) * 4 + r16 // 4) * per_res

    for src, dst in ((q_ref, qp_ref), (k_ref, kp_ref), (v_ref, vp_ref)):
        stage_ref[...] = src[...].astype(f32)
        for r in range(RES16):
            dst[pl.ds(perm_row0(r), per_res), :] = stage_ref[pl.ds(r, per_res, stride=RES16), :].astype(bf16)

    zpad = jnp.zeros((HALF, HEAD_DIM), bf16)
    kpad_ref[pl.ds(0, HALF), :] = zpad
    vpad_ref[pl.ds(0, HALF), :] = zpad
    kpad_ref[pl.ds(HALF + seq, HALF), :] = zpad
    vpad_ref[pl.ds(HALF + seq, HALF), :] = zpad
    kpad_ref[pl.ds(HALF, seq), :] = k_ref[...]
    vpad_ref[pl.ds(HALF, seq), :] = v_ref[...]

    for r4 in range(4):
        base = r4 * cls
        q4 = qp_ref[pl.ds(base, cls), :]
        k4 = kp_ref[pl.ds(base, cls), :]
        v4 = vp_ref[pl.ds(base, cls), :]
        raw = lax.dot_general(q4, k4, dn, preferred_element_type=f32)
        m2, p2, d2 = _softmax_parts(raw + b2_ref[...])
        o2 = jnp.dot(p2.astype(bf16), v4, preferred_element_type=f32)
        m3s, d3s, o3s = [], [], []
        for jq in range(4):
            lo, hi = jq * per_res, (jq + 1) * per_res
            m3, p3, d3 = _softmax_parts(raw[lo:hi, lo:hi] + b3_ref[...])
            m3s.append(m3)
            d3s.append(d3)
            o3s.append(jnp.dot(p3.astype(bf16), v4[lo:hi], preferred_element_type=f32))
        m3 = jnp.concatenate(m3s, axis=0)
        d3 = jnp.concatenate(d3s, axis=0)
        o3 = jnp.concatenate(o3s, axis=0)
        m23 = jnp.maximum(m2, m3)
        w2 = jnp.exp(m2 - m23)
        w3 = jnp.exp(m3 - m23)
        d23 = w2 * d2 + w3 * d3
        o23_ref[pl.ds(base, cls), :] = (w2 * o2 + w3 * o3) / d23
        l23_ref[pl.ds(base, cls), :] = jnp.broadcast_to(m23 + jnp.log(d23), (cls, HEAD_DIM))

    for r in range(RES16):
        stage_ref[pl.ds(r, per_res, stride=RES16), :] = o23_ref[pl.ds(perm_row0(r), per_res), :]
    for r in range(RES16):
        o23_ref[pl.ds(r, per_res, stride=RES16), :] = l23_ref[pl.ds(perm_row0(r), per_res), :]

    def nat_block(n, carry):
        row0 = pl.multiple_of(n * qb, qb)
        q = q_ref[pl.ds(row0, qb), :]
        kw = kpad_ref[pl.ds(row0, qb + 2 * HALF), :]
        vw = vpad_ref[pl.ds(row0, qb + 2 * HALF), :]
        which = jnp.where(n == 0, 1, jnp.where(n == nblk - 1, 2, 0))
        s1 = lax.dot_general(q, kw, dn, preferred_element_type=f32) + b1_ref[which]
        m1, p1, d1 = _softmax_parts(s1)
        o1 = jnp.dot(p1.astype(bf16), vw, preferred_element_type=f32)
        l23 = o23_ref[pl.ds(row0, qb), :]
        on23 = stage_ref[pl.ds(row0, qb), :]
        mm = jnp.maximum(m1, l23)
        w1 = jnp.exp(m1 - mm)
        w23 = jnp.exp(l23 - mm)
        o_ref[pl.ds(row0, qb), :] = ((w1 * o1 + w23 * on23) / (w1 * d1 + w23)).astype(o_ref.dtype)
        return carry

    lax.fori_loop(0, nblk, nat_block, 0, unroll=2)


def _dilated_attention(proj3, b1, b2, b3):
    nb, seq, _ = proj3.shape
    assert seq // RES16 == Q_BLOCK and seq % NAT_BLOCK == 0
    q0 = 2 * D_LRU // HEAD_DIM
    k0 = q0 + N_HEADS
    v0 = k0 + N_HEADS
    return pl.pallas_call(
        _attn_kernel,
        out_shape=jax.ShapeDtypeStruct((nb, seq, D_ATTN), bf16),
        grid=(N_HEADS, nb),
        in_specs=[
            pl.BlockSpec((None, seq, HEAD_DIM), lambda h, b: (b, 0, q0 + h)),
            pl.BlockSpec((None, seq, HEAD_DIM), lambda h, b: (b, 0, k0 + h)),
            pl.BlockSpec((None, seq, HEAD_DIM), lambda h, b: (b, 0, v0 + h)),
            pl.BlockSpec((None, 3, NAT_BLOCK, NAT_BLOCK + 2 * HALF), lambda h, b: (h, 0, 0, 0)),
            pl.BlockSpec((None, 4 * Q_BLOCK, 4 * Q_BLOCK), lambda h, b: (h, 0, 0)),
            pl.BlockSpec((None, Q_BLOCK, Q_BLOCK), lambda h, b: (h, 0, 0)),
        ],
        out_specs=pl.BlockSpec((None, seq, HEAD_DIM), lambda h, b: (b, 0, h)),
        scratch_shapes=[
            pltpu.VMEM((seq, HEAD_DIM), f32),
            pltpu.VMEM((seq, HEAD_DIM), bf16),
            pltpu.VMEM((seq, HEAD_DIM), bf16),
            pltpu.VMEM((seq, HEAD_DIM), bf16),
            pltpu.VMEM((seq + 2 * HALF, HEAD_DIM), bf16),
            pltpu.VMEM((seq + 2 * HALF, HEAD_DIM), bf16),
            pltpu.VMEM((seq, HEAD_DIM), f32),
            pltpu.VMEM((seq, HEAD_DIM), f32),
        ],
        compiler_params=_cparams(("parallel", "parallel"), VMEM_LIMIT_MID),
        name="dilated_attn",
    )(proj3, proj3, proj3, b1, b2, b3)


def _layer_norm(z, g, b):
    mu = jnp.mean(z, axis=-1, keepdims=True)
    zc = z - mu
    var = jnp.mean(zc * zc, axis=-1, keepdims=True)
    return zc * lax.rsqrt(var + LN_EPS) * g + b


def _rms_norm(v, g):
    return v * lax.rsqrt(jnp.mean(v * v, axis=-1, keepdims=True) + LN_EPS) * g


def _pack_bf16_pairs(v):
    n = v.shape[1] // 2
    lo = pltpu.bitcast(v[:, :n].astype(bf16).astype(f32), u32)
    hi = pltpu.bitcast(v[:, n:].astype(bf16).astype(f32), u32)
    return (lo >> 16) | (hi & jnp.uint32(0xFFFF0000))


def _unpack_bf16_pairs(p):
    lo = pltpu.bitcast(p << 16, f32)
    hi = pltpu.bitcast(p & jnp.uint32(0xFFFF0000), f32)
    return jnp.concatenate([lo.astype(bf16), hi.astype(bf16)], axis=1)


LANES = 128
TOK_SUB = D_MODEL // 2 // LANES


def _store_token_tiles(ref, packed, row0=0):
    rows = packed.shape[0]
    for c in range(TOK_SUB):
        ref[pl.ds(TOK_SUB * row0 + c, rows, stride=TOK_SUB), :] = packed[:, c * LANES:(c + 1) * LANES]


def _load_token_tiles_bf16(ref, rows, row0=0):
    los, his = [], []
    for c in range(TOK_SUB):
        p = ref[pl.ds(TOK_SUB * row0 + c, rows, stride=TOK_SUB), :]
        los.append(pltpu.bitcast(p << 16, f32).astype(bf16))
        his.append(pltpu.bitcast(p & jnp.uint32(0xFFFF0000), f32).astype(bf16))
    return jnp.concatenate(los + his, axis=1)


def _outproj_kernel(lru_ref, att_ref, x_ref, gl_ref, ga_ref, w_ref, gm_ref, shf_ref, scf_ref,
                    lng_ref, lnb_ref, wrh_ref, wrl_ref, br_ref,
                    x1_ref, hp_ref, te_ref, tw_ref, rk_ref, cnt_ref, carry_ref):
    tm = x_ref.shape[0]
    step = pl.program_id(0)

    @pl.when(step == 0)
    def _():
        carry_ref[...] = jnp.zeros_like(carry_ref)

    ln = _rms_norm(lru_ref[...].astype(f32), gl_ref[...]).astype(bf16)
    an = _rms_norm(att_ref[...].astype(f32), ga_ref[...]).astype(bf16)
    y = (jnp.dot(ln, w_ref[pl.ds(0, D_LRU), :], preferred_element_type=f32)
         + jnp.dot(an, w_ref[pl.ds(D_LRU, D_ATTN), :], preferred_element_type=f32))
    x1 = _layer_norm(DEEPNORM_ALPHA * x_ref[...] + gm_ref[...] * y, lng_ref[...], lnb_ref[...])
    x1_ref[...] = x1
    hf = x1 * (1.0 + scf_ref[...]) + shf_ref[...]
    _store_token_tiles(hp_ref, _pack_bf16_pairs(hf))

    hf_hi = hf.astype(bf16)
    hf_lo = (hf - hf_hi.astype(f32)).astype(bf16)
    logits = (jnp.dot(hf_hi, wrh_ref[...], preferred_element_type=f32)
              + (jnp.dot(hf_lo, wrh_ref[...], preferred_element_type=f32)
                 + jnp.dot(hf_hi, wrl_ref[...], preferred_element_type=f32))
              + br_ref[...])
    lane = lax.broadcasted_iota(i32, (tm, N_EXPERTS), 1)
    col4 = lax.broadcasted_iota(i32, (tm, TOP_K), 1)
    work = logits
    vals, idxs = [], []
    for _k in range(TOP_K):
        m = jnp.max(work, axis=-1, keepdims=True)
        idx = jnp.min(jnp.where(work == m, lane, N_EXPERTS), axis=-1, keepdims=True)
        vals.append(m)
        idxs.append(idx)
        work = jnp.where(lane == idx, -jnp.inf, work)
    exps = [jnp.exp(v - vals[0]) for v in vals]
    den = exps[0] + exps[1] + exps[2] + exps[3]

    onehot = jnp.zeros((tm, N_EXPERTS), f32)
    for idx in idxs:
        onehot = onehot + (lane == idx).astype(f32)
    ri = lax.broadcasted_iota(i32, (tm, tm), 0)
    ci = lax.broadcasted_iota(i32, (tm, tm), 1)
    lower = (ri > ci).astype(bf16)
    before = jnp.dot(lower, onehot.astype(bf16), preferred_element_type=f32) + carry_ref[...]
    te = jnp.zeros((tm, TOP_K), i32)
    tw = jnp.zeros((tm, TOP_K), f32)
    rk = jnp.zeros((tm, TOP_K), i32)
    for k in range(TOP_K):
        rank_k = jnp.sum(jnp.where(lane == idxs[k], before, 0.0), axis=-1, keepdims=True)
        te = jnp.where(col4 == k, idxs[k], te)
        tw = jnp.where(col4 == k, exps[k] / den, tw)
        rk = jnp.where(col4 == k, rank_k.astype(i32), rk)
    te_ref[...] = te
    tw_ref[...] = tw
    rk_ref[...] = rk
    carry_ref[...] = carry_ref[...] + jnp.sum(onehot, axis=0, keepdims=True)
    cnt_ref[...] = carry_ref[...].astype(i32)


def _out_proj_router(lru2, att2, x2, g_lru, g_attn, w_out_bf, mod3, ln_g, ln_b, w_router, b_router, seq):
    t, d = x2.shape
    tm = TOK_TILE
    tps = seq // tm
    row = lambda i: (i, 0)
    const = lambda i: (0, 0)
    wr_hi = w_router.astype(bf16)
    wr_lo = (w_router - wr_hi.astype(f32)).astype(bf16)
    return pl.pallas_call(
        _outproj_kernel,
        out_shape=(
            jax.ShapeDtypeStruct((t, d), f32),
            jax.ShapeDtypeStruct((t * TOK_SUB, LANES), u32),
            jax.ShapeDtypeStruct((t, TOP_K), i32),
            jax.ShapeDtypeStruct((t, TOP_K), f32),
            jax.ShapeDtypeStruct((t, TOP_K), i32),
            jax.ShapeDtypeStruct((1, N_EXPERTS), i32),
        ),
        grid=(t // tm,),
        in_specs=[
            pl.BlockSpec((tm, D_LRU), row),
            pl.BlockSpec((tm, D_ATTN), row),
            pl.BlockSpec((tm, d), row),
            pl.BlockSpec((1, D_LRU), const),
            pl.BlockSpec((1, D_ATTN), const),
            pl.BlockSpec((d, d), const),
            pl.BlockSpec((None, 1, d), lambda i: (i // tps, 0, 2)),
            pl.BlockSpec((None, 1, d), lambda i: (i // tps, 0, 3)),
            pl.BlockSpec((None, 1, d), lambda i: (i // tps, 0, 4)),
            pl.BlockSpec((1, d), const),
            pl.BlockSpec((1, d), const),
            pl.BlockSpec((d, N_EXPERTS), const),
            pl.BlockSpec((d, N_EXPERTS), const),
            pl.BlockSpec((1, N_EXPERTS), const),
        ],
        out_specs=(
            pl.BlockSpec((tm, d), row),
            pl.BlockSpec((tm * TOK_SUB, LANES), row),
            pl.BlockSpec((tm, TOP_K), row),
            pl.BlockSpec((tm, TOP_K), row),
            pl.BlockSpec((tm, TOP_K), row),
            pl.BlockSpec((1, N_EXPERTS), const),
        ),
        scratch_shapes=[pltpu.VMEM((1, N_EXPERTS), f32)],
        compiler_params=_cparams(("arbitrary",), VMEM_LIMIT_BIG),
        name="out_proj_router",
    )(lru2, att2, x2, g_lru.reshape(1, -1), g_attn.reshape(1, -1), w_out_bf, mod3, mod3, mod3,
      ln_g.reshape(1, -1), ln_b.reshape(1, -1), wr_hi, wr_lo, b_router.reshape(1, -1))


def _dispatch_kernel(dest_ref, pad_start_ref, pad_len_ref, nt_ref, hp_ref, xs_ref, zero_ref, sem):
    tm = hp_ref.shape[0] // TOK_SUB
    tile_rows = EXP_TILE * TOK_SUB
    n_tiles = xs_ref.shape[0] // tile_rows

    def token_tile(ref, r):
        return ref.at[pl.ds(pl.multiple_of(r * TOK_SUB, TOK_SUB), TOK_SUB), :]

    def copy(tok, k):
        return pltpu.make_async_copy(token_tile(hp_ref, tok), token_tile(xs_ref, dest_ref[tok * TOP_K + k]), sem)

    def start(tok, c):
        for k in range(TOP_K):
            copy(tok, k).start()
        return c

    def wait(tok, c):
        for k in range(TOP_K):
            copy(tok, k).wait()
        return c

    lax.fori_loop(0, tm, start, 0, unroll=2)
    lax.fori_loop(0, tm, wait, 0, unroll=2)

    @pl.when(pl.program_id(0) == pl.num_programs(0) - 1)
    def _():
        zero_ref[...] = jnp.zeros_like(zero_ref)

        def per_expert(e, c):
            p0 = pad_start_ref[e]
            n = pad_len_ref[e]

            def zcopy(r):
                return pltpu.make_async_copy(token_tile(zero_ref, 0), token_tile(xs_ref, p0 + r), sem)

            lax.fori_loop(0, n, lambda r, c2: (zcopy(r).start(), c2)[1], 0)
            lax.fori_loop(0, n, lambda r, c2: (zcopy(r).wait(), c2)[1], 0)
            return c

        lax.fori_loop(0, N_EXPERTS, per_expert, 0)

        def tail_tile(m, c):
            cp = pltpu.make_async_copy(
                zero_ref, xs_ref.at[pl.ds(pl.multiple_of(m * tile_rows, tile_rows), tile_rows), :], sem)
            cp.start()
            cp.wait()
            return c

        lax.fori_loop(nt_ref[0], n_tiles, tail_tile, 0)


def _dispatch(hp, dest_flat, pad_start, pad_len, n_tiles_used, n_rows):
    t = hp.shape[0] // TOK_SUB
    tm = TOK_TILE
    smem = pl.BlockSpec(memory_space=pltpu.SMEM)
    return pl.pallas_call(
        _dispatch_kernel,
        out_shape=jax.ShapeDtypeStruct((n_rows * TOK_SUB, LANES), u32),
        grid=(t // tm,),
        in_specs=[
            pl.BlockSpec((tm * TOP_K,), lambda i: (i,), memory_space=pltpu.SMEM),
            smem, smem, smem,
            pl.BlockSpec((tm * TOK_SUB, LANES), lambda i: (i, 0)),
        ],
        out_specs=pl.BlockSpec(memory_space=pl.ANY),
        scratch_shapes=[pltpu.VMEM((EXP_TILE * TOK_SUB, LANES), u32), pltpu.SemaphoreType.DMA(())],
        compiler_params=_cparams(("arbitrary",), VMEM_LIMIT_MID),
        name="moe_dispatch",
    )(dest_flat, pad_start, pad_len, n_tiles_used, hp)


def _gate_up_kernel(te_ref, nt_ref, xs_ref, wg_ref, wu_ref, bg_ref, bu_ref, h_ref, w_scr):
    m = pl.program_id(1)
    fc = wg_ref.shape[1]
    changed = jnp.logical_or(m == 0, te_ref[m] != te_ref[jnp.maximum(m - 1, 0)])

    @pl.when(jnp.logical_and(changed, m < nt_ref[0]))
    def _():
        w_scr[:, pl.ds(0, fc)] = wg_ref[...].astype(bf16)
        w_scr[:, pl.ds(fc, fc)] = wu_ref[...].astype(bf16)

    @pl.when(m < nt_ref[0])
    def _():
        x = _load_token_tiles_bf16(xs_ref, EXP_TILE)
        gu = jnp.dot(x, w_scr[...], preferred_element_type=f32)
        gate = jnp.minimum(gu[:, :fc] + bg_ref[...], SWIGLU_LIMIT)
        up = jnp.clip(gu[:, fc:] + bu_ref[...], -SWIGLU_LIMIT, SWIGLU_LIMIT)
        glu = gate * jax.nn.sigmoid(gate * SWIGLU_ALPHA)
        h_ref[...] = _pack_bf16_pairs((up + 1.0) * glu)

    @pl.when(m >= nt_ref[0])
    def _():
        h_ref[...] = jnp.zeros_like(h_ref)


def _gate_up(xs, w_gate_up, b_gate_up4, layer, tile_e, n_tiles_used):
    p = xs.shape[0] // TOK_SUB
    d = D_MODEL
    fc = D_FF // 2
    n_pass = D_FF // fc
    tm = EXP_TILE
    grid_spec = pltpu.PrefetchScalarGridSpec(
        num_scalar_prefetch=2,
        grid=(n_pass, p // tm),
        in_specs=[
            pl.BlockSpec((tm * TOK_SUB, LANES), lambda j, m, te, nt: (m, 0)),
            pl.BlockSpec((None, None, d, fc), lambda j, m, te, nt: (layer, te[m], 0, j)),
            pl.BlockSpec((None, None, d, fc), lambda j, m, te, nt: (layer, te[m], 0, n_pass + j)),
            pl.BlockSpec((None, None, 1, fc), lambda j, m, te, nt: (layer, te[m], 0, j)),
            pl.BlockSpec((None, None, 1, fc), lambda j, m, te, nt: (layer, te[m], 0, n_pass + j)),
        ],
        out_specs=pl.BlockSpec((tm, fc // 2), lambda j, m, te, nt: (m, j)),
        scratch_shapes=[pltpu.VMEM((d, 2 * fc), bf16)],
    )
    return pl.pallas_call(
        _gate_up_kernel,
        out_shape=jax.ShapeDtypeStruct((p, D_FF // 2), u32),
        grid_spec=grid_spec,
        compiler_params=_cparams(("arbitrary", "arbitrary"), VMEM_LIMIT_BIG),
        name="moe_gate_up",
    )(tile_e, n_tiles_used, xs, w_gate_up, w_gate_up, b_gate_up4, b_gate_up4)


def _down_kernel(te_ref, nt_ref, h_ref, w_ref, b_ref, y_ref, w_scr):
    m = pl.program_id(0)
    fc = D_FF // 2
    changed = jnp.logical_or(m == 0, te_ref[m] != te_ref[jnp.maximum(m - 1, 0)])

    @pl.when(jnp.logical_and(changed, m < nt_ref[0]))
    def _():
        w_scr[...] = w_ref[...].astype(bf16)

    @pl.when(m < nt_ref[0])
    def _():
        hp = h_ref[...]
        h = jnp.concatenate([_unpack_bf16_pairs(hp[:, :fc // 2]), _unpack_bf16_pairs(hp[:, fc // 2:])], axis=1)
        y = jnp.dot(h, w_scr[...], preferred_element_type=f32) + b_ref[...]
        _store_token_tiles(y_ref, _pack_bf16_pairs(y))

    @pl.when(m >= nt_ref[0])
    def _():
        y_ref[...] = jnp.zeros_like(y_ref)


def _down(hs, w_down, b_down4, layer, tile_e, n_tiles_used):
    p, half = hs.shape
    d_ff = 2 * half
    d = w_down.shape[3]
    tm = EXP_TILE
    grid_spec = pltpu.PrefetchScalarGridSpec(
        num_scalar_prefetch=2,
        grid=(p // tm,),
        in_specs=[
            pl.BlockSpec((tm, half), lambda m, te, nt: (m, 0)),
            pl.BlockSpec((None, None, d_ff, d), lambda m, te, nt: (layer, te[m], 0, 0)),
            pl.BlockSpec((None, None, 1, d), lambda m, te, nt: (layer, te[m], 0, 0)),
        ],
        out_specs=pl.BlockSpec((tm * TOK_SUB, LANES), lambda m, te, nt: (m, 0)),
        scratch_shapes=[pltpu.VMEM((d_ff, d), bf16)],
    )
    return pl.pallas_call(
        _down_kernel,
        out_shape=jax.ShapeDtypeStruct((p * TOK_SUB, LANES), u32),
        grid_spec=grid_spec,
        compiler_params=_cparams(("arbitrary",), VMEM_LIMIT_BIG),
        name="moe_down",
    )(tile_e, n_tiles_used, hs, w_down, b_down4)


def _combine_kernel(dest_ref, ys_ref, tw_ref, x1_ref, gf_ref, lng_ref, lnb_ref, o_ref, buf_ref, sem):
    tm = x1_ref.shape[0]

    def token_tile(ref, r):
        return ref.at[pl.ds(pl.multiple_of(r * TOK_SUB, TOK_SUB), TOK_SUB), :]

    def copy(tok, k):
        return pltpu.make_async_copy(token_tile(ys_ref, dest_ref[tok * TOP_K + k]),
                                     token_tile(buf_ref, k * tm + tok), sem)

    def start(tok, c):
        for k in range(TOP_K):
            copy(tok, k).start()
        return c

    def wait(tok, c):
        for k in range(TOP_K):
            copy(tok, k).wait()
        return c

    lax.fori_loop(0, tm, start, 0, unroll=2)
    lax.fori_loop(0, tm, wait, 0, unroll=2)

    tw = tw_ref[...]
    y = jnp.zeros((tm, D_MODEL), f32)
    for k in range(TOP_K):
        y = y + tw[:, k:k + 1] * _load_token_tiles_bf16(buf_ref, tm, row0=k * tm).astype(f32)
    o_ref[...] = _layer_norm(DEEPNORM_ALPHA * x1_ref[...] + gf_ref[...] * y, lng_ref[...], lnb_ref[...])


def _combine(ys, dest_flat, top_w, x1, mod3, ln_g, ln_b, seq):
    t, d = x1.shape
    tm = TOK_TILE
    tps = seq // tm
    grid_spec = pltpu.PrefetchScalarGridSpec(
        num_scalar_prefetch=0,
        grid=(t // tm,),
        in_specs=[
            pl.BlockSpec((tm * TOP_K,), lambda i: (i,), memory_space=pltpu.SMEM),
            pl.BlockSpec(memory_space=pl.ANY),
            pl.BlockSpec((tm, TOP_K), lambda i: (i, 0)),
            pl.BlockSpec((tm, d), lambda i: (i, 0)),
            pl.BlockSpec((None, 1, d), lambda i: (i // tps, 0, 5)),
            pl.BlockSpec((1, d), lambda i: (0, 0)),
            pl.BlockSpec((1, d), lambda i: (0, 0)),
        ],
        out_specs=pl.BlockSpec((tm, d), lambda i: (i, 0)),
        scratch_shapes=[pltpu.VMEM((TOP_K * tm * TOK_SUB, LANES), u32), pltpu.SemaphoreType.DMA(())],
    )
    return pl.pallas_call(
        _combine_kernel,
        out_shape=jax.ShapeDtypeStruct((t, d), f32),
        grid_spec=grid_spec,
        compiler_params=_cparams(("arbitrary",), VMEM_LIMIT_MID),
        name="moe_combine",
    )(dest_flat, ys, top_w, x1, mod3, ln_g.reshape(1, -1), ln_b.reshape(1, -1))


def _routing_tables(top_e, rank, counts, n_tiles):
    tm = EXP_TILE
    counts = counts.reshape(-1)
    padded = (counts + tm - 1) // tm * tm
    pend = jnp.cumsum(padded)
    pstart = pend - padded
    dest = (pstart[top_e] + rank).reshape(-1).astype(i32)
    tile_start = jnp.arange(n_tiles, dtype=i32) * tm
    tile_e = jnp.sum((pend[None, :] <= tile_start[:, None]).astype(i32), axis=1)
    tile_e = jnp.minimum(tile_e, N_EXPERTS - 1).astype(i32)
    n_used = (pend[-1] // tm).astype(i32).reshape(1)
    return dest, tile_e, n_used, (pstart + counts).astype(i32), (padded - counts).astype(i32)


def kernel(x, c, rel_bias, w_ada, b_ada, w_in, conv_w, conv_b, lru_w_a, lru_b_a, lru_w_x, lru_b_x, lru_lambda, norm_lru_g, norm_attn_g, w_out, ln_mix_g, ln_mix_b, w_router, b_router, w_gate_up, b_gate_up, w_down, b_down, ln_ffn_g, ln_ffn_b):
    nb, seq, d = x.shape
    t = nb * seq
    mod = _ada_mod(c, w_ada, b_ada)
    b1, b2, b3 = _attn_bias_tables(rel_bias)
    n_tiles = t * TOP_K // EXP_TILE + N_EXPERTS
    q_lo = 2 * D_LRU
    x2 = x.reshape(t, d)
    for l in range(DEPTH):
        mod3 = mod[l].reshape(nb, 1, 6 * d)
        w_in_l = w_in[l].at[:, q_lo:q_lo + D_ATTN].multiply(HEAD_DIM ** -0.5).astype(bf16)
        proj = _in_proj(x2, mod3, w_in_l, seq).reshape(nb, seq, D_IN)
        wg = jnp.concatenate([lru_w_a[l], lru_w_x[l]], axis=-1).astype(bf16)
        bg = jnp.concatenate([lru_b_a[l].reshape(2, N_LRU_BLOCKS, 1, LRU_BLOCK),
                              lru_b_x[l].reshape(2, N_LRU_BLOCKS, 1, LRU_BLOCK)], axis=-1)
        lru = _lru_group(proj, conv_w[l], conv_b[l], wg, bg, lru_lambda[l])
        att = _dilated_attention(proj, b1, b2, b3)
        x1, hp, top_e, top_w, rank, counts = _out_proj_router(
            lru.reshape(t, D_LRU), att.reshape(t, D_ATTN), x2, norm_lru_g[l], norm_attn_g[l],
            w_out[l].astype(bf16), mod3, ln_mix_g[l], ln_mix_b[l], w_router[l], b_router[l], seq)
        dest, tile_e, n_used, pad_start, pad_len = _routing_tables(top_e, rank, counts, n_tiles)
        xs = _dispatch(hp, dest, pad_start, pad_len, n_used, n_tiles * EXP_TILE)
        hs = _gate_up(xs, w_gate_up, b_gate_up.reshape(DEPTH, N_EXPERTS, 1, 2 * D_FF), l, tile_e, n_used)
        ys = _down(hs, w_down, b_down.reshape(DEPTH, N_EXPERTS, 1, d), l, tile_e, n_used)
        x2 = _combine(ys, dest, top_w, x1, mod3, ln_ffn_g[l], ln_ffn_b[l], seq)
    return x2.reshape(nb, seq, d)
```

```python
import functools
import math

import jax
import jax.numpy as jnp
from jax import lax
from jax.experimental import pallas as pl
from jax.experimental.pallas import tpu as pltpu

f32 = jnp.float32
bf16 = jnp.bfloat16
i32 = jnp.int32
u32 = jnp.uint32

D_MODEL = 2048
DEPTH = 2
D_LRU = 1024
N_LRU_BLOCKS = 8
LRU_BLOCK = 128
CONV_WIDTH = 4
LRU_C = 8.0
D_ATTN = 1024
HEAD_DIM = 128
N_HEADS = 8
D_IN = 2 * D_LRU + 3 * D_ATTN
NEG_INF = -1e30
N_BUCKETS = 32
MAX_EXACT = 8
MAX_DISTANCE = 1024
N_EXPERTS = 32
TOP_K = 4
D_FF = D_MODEL
SWIGLU_LIMIT = 7.0
SWIGLU_ALPHA = 1.702
DEEPNORM_ALPHA = (2 * DEPTH) ** 0.25
LN_EPS = 1e-5

VMEM_LIMIT_BIG = 56 * 1024 * 1024
VMEM_LIMIT_MID = 40 * 1024 * 1024

ROW_TILE = 512
TOK_TILE = 256
EXP_TILE = 512
LRU_CHUNK = 128
Q_BLOCK = 128
NAT_BLOCK = 256
HALF = 64
RES16 = 16


def _cparams(sem, vmem=None):
    return pltpu.CompilerParams(dimension_semantics=sem, vmem_limit_bytes=vmem)


def _ada_kernel(c_ref, w_ref, b_ref, o_ref):
    c = c_ref[...]
    ca = c * jax.nn.sigmoid(c)
    o_ref[0] = jnp.dot(ca, w_ref[0], precision=lax.Precision.HIGHEST,
                       preferred_element_type=f32) + b_ref[0]


def _ada_mod(c, w_ada, b_ada):
    depth, d, n = w_ada.shape
    b = c.shape[0]
    tn = 1024
    return pl.pallas_call(
        _ada_kernel,
        out_shape=jax.ShapeDtypeStruct((depth, b, n), f32),
        grid=(depth, n // tn),
        in_specs=[
            pl.BlockSpec((b, d), lambda l, j: (0, 0)),
            pl.BlockSpec((1, d, tn), lambda l, j: (l, 0, j)),
            pl.BlockSpec((1, 1, tn), lambda l, j: (l, 0, j)),
        ],
        out_specs=pl.BlockSpec((1, b, tn), lambda l, j: (l, 0, j)),
        compiler_params=_cparams(("parallel", "parallel"), VMEM_LIMIT_MID),
        name="ada_mod",
    )(c, w_ada, b_ada.reshape(depth, 1, n))


def _inproj_kernel(x_ref, sh_ref, sc_ref, w_ref, o_ref):
    h = x_ref[...] * (1.0 + sc_ref[...]) + sh_ref[...]
    o_ref[...] = jnp.dot(h.astype(bf16), w_ref[...], preferred_element_type=f32).astype(o_ref.dtype)


def _in_proj(x2, mod3, w_in_bf, seq):
    t, d = x2.shape
    n = w_in_bf.shape[1]
    tn = n // 2
    tiles_per_seq = seq // ROW_TILE
    return pl.pallas_call(
        _inproj_kernel,
        out_shape=jax.ShapeDtypeStruct((t, n), bf16),
        grid=(n // tn, t // ROW_TILE),
        in_specs=[
            pl.BlockSpec((ROW_TILE, d), lambda j, i: (i, 0)),
            pl.BlockSpec((None, 1, d), lambda j, i: (i // tiles_per_seq, 0, 0)),
            pl.BlockSpec((None, 1, d), lambda j, i: (i // tiles_per_seq, 0, 1)),
            pl.BlockSpec((d, tn), lambda j, i: (0, j)),
        ],
        out_specs=pl.BlockSpec((ROW_TILE, tn), lambda j, i: (i, j)),
        compiler_params=_cparams(("parallel", "parallel"), VMEM_LIMIT_BIG),
        name="in_proj",
    )(x2, mod3, mod3, w_in_bf)


def _lru_kernel(xb_ref, gb_ref, cw_ref, cb_ref, wg_ref, bg_ref, lam_ref, o_ref,
                xt_ref, hs_ref, a_ref, b_ref):
    nb, seq, _ = xb_ref.shape
    tc = LRU_CHUNK
    rows = tc * nb
    n_chunks = seq // tc
    front = (CONV_WIDTH // 2) * nb

    xt_ref[pl.ds(0, front), :] = jnp.zeros((front, LRU_BLOCK), f32)
    xt_ref[pl.ds(front + seq * nb, nb), :] = jnp.zeros((nb, LRU_BLOCK), f32)
    for b in range(nb):
        xt_ref[pl.ds(front + b, seq, stride=nb), :] = xb_ref[b].astype(f32)

    lam = lam_ref[...]
    neg_c_sp = -LRU_C * jax.nn.softplus(-lam)
    cw = cw_ref[...]
    cb = cb_ref[...]

    def gates(d, r0):
        xc = cb
        for j in range(CONV_WIDTH):
            xc = xc + cw[j:j + 1, :] * xt_ref[pl.ds(pl.multiple_of(r0 + j * nb, nb), rows), :]
        g = jnp.dot(xc.astype(bf16), wg_ref[d], preferred_element_type=f32) + bg_ref[d]
        r = 0.5 + 0.5 * jnp.tanh(0.5 * g[:, :LRU_BLOCK])
        i = 0.5 + 0.5 * jnp.tanh(0.5 * g[:, LRU_BLOCK:])
        log_a = neg_c_sp[d:d + 1, :] * r
        a = jnp.exp(log_a)
        u = a * a
        z = 2.0 * log_a
        one_minus = jnp.where(u == 1.0, -z, (1.0 - u) * z / jnp.log(u))
        root = jnp.where(one_minus > 0.0, one_minus * lax.rsqrt(one_minus), 0.0)
        a_ref[...] = a
        b_ref[...] = root * (i * xc)

    def fwd_chunk(c, h):
        r0 = pl.multiple_of(c * rows, rows)
        gates(0, r0)

        def step(t, h):
            rr = pl.multiple_of(t * nb, nb)
            h = a_ref[pl.ds(rr, nb), :] * h + b_ref[pl.ds(rr, nb), :]
            hs_ref[pl.ds(r0 + rr, nb), :] = h
            return h

        return lax.fori_loop(0, tc, step, h, unroll=8)

    lax.fori_loop(0, n_chunks, fwd_chunk, jnp.zeros((nb, LRU_BLOCK), f32))

    def bwd_chunk(ci, h):
        c = n_chunks - 1 - ci
        r0 = pl.multiple_of(c * rows, rows)
        gates(1, r0)

        def step(ti, h):
            rr = pl.multiple_of((tc - 1 - ti) * nb, nb)
            h = a_ref[pl.ds(rr, nb), :] * h + b_ref[pl.ds(rr, nb), :]
            b_ref[pl.ds(rr, nb), :] = h
            return h

        h = lax.fori_loop(0, tc, step, h, unroll=8)
        hs_ref[pl.ds(r0, rows), :] = hs_ref[pl.ds(r0, rows), :] + b_ref[...]
        t0 = pl.multiple_of(c * tc, tc)
        for b in range(nb):
            hr = hs_ref[pl.ds(r0 + b, tc, stride=nb), :]
            gate = jax.nn.gelu(gb_ref[b, pl.ds(t0, tc), :].astype(f32))
            o_ref[b, pl.ds(t0, tc), :] = (hr * gate).astype(o_ref.dtype)
        return h

    lax.fori_loop(0, n_chunks, bwd_chunk, jnp.zeros((nb, LRU_BLOCK), f32))


def _lru_group(proj3, conv_w, conv_b, wg, bg, lam):
    nb, seq, _ = proj3.shape
    g = N_LRU_BLOCKS
    return pl.pallas_call(
        _lru_kernel,
        out_shape=jax.ShapeDtypeStruct((nb, seq, D_LRU), bf16),
        grid=(g,),
        in_specs=[
            pl.BlockSpec((nb, seq, LRU_BLOCK), lambda j: (0, 0, j)),
            pl.BlockSpec((nb, seq, LRU_BLOCK), lambda j: (0, 0, g + j)),
            pl.BlockSpec((CONV_WIDTH, LRU_BLOCK), lambda j: (0, j)),
            pl.BlockSpec((1, LRU_BLOCK), lambda j: (0, j)),
            pl.BlockSpec((2, None, LRU_BLOCK, 2 * LRU_BLOCK), lambda j: (0, j, 0, 0)),
            pl.BlockSpec((2, None, 1, 2 * LRU_BLOCK), lambda j: (0, j, 0, 0)),
            pl.BlockSpec((2, LRU_BLOCK), lambda j: (0, j)),
        ],
        out_specs=pl.BlockSpec((nb, seq, LRU_BLOCK), lambda j: (0, 0, j)),
        scratch_shapes=[
            pltpu.VMEM(((seq + CONV_WIDTH - 1) * nb, LRU_BLOCK), f32),
            pltpu.VMEM((seq * nb, LRU_BLOCK), f32),
            pltpu.VMEM((LRU_CHUNK * nb, LRU_BLOCK), f32),
            pltpu.VMEM((LRU_CHUNK * nb, LRU_BLOCK), f32),
        ],
        compiler_params=_cparams(("parallel",), VMEM_LIMIT_BIG),
        name="rg_lru",
    )(proj3, proj3, conv_w, conv_b.reshape(1, D_LRU), wg, bg, lam)


def _t5_bucket(rel):
    nbk = N_BUCKETS // 2
    ret = jnp.where(rel > 0, nbk, 0)
    n = jnp.abs(rel)
    nf = jnp.maximum(n, 1).astype(f32)
    large = MAX_EXACT + (jnp.log(nf / MAX_EXACT) / math.log(MAX_DISTANCE / MAX_EXACT)
                         * (nbk - MAX_EXACT)).astype(i32)
    large = jnp.minimum(large, nbk - 1)
    return ret + jnp.where(n < MAX_EXACT, n, large)


def _attn_bias_tables(rel_bias):
    qb = Q_BLOCK
    rb = rel_bias.astype(f32)
    i = jnp.arange(qb, dtype=i32)[:, None]

    def tile(step, dilation, valid):
        onehot = jax.nn.one_hot(_t5_bucket(step * dilation), N_BUCKETS, dtype=f32)
        t = jnp.einsum("qkn,nh->hqk", onehot, rb, precision=lax.Precision.HIGHEST)
        return jnp.where(valid[None], t, NEG_INF)

    i1 = jnp.arange(NAT_BLOCK, dtype=i32)[:, None]
    win = NAT_BLOCK + 2 * HALF
    j = jnp.arange(win, dtype=i32)[None, :]
    step1 = j - HALF - i1
    band1 = jnp.abs(step1) <= HALF
    b1 = jnp.stack([tile(step1, 1, band1),
                    tile(step1, 1, band1 & (j >= HALF)),
                    tile(step1, 1, band1 & (j < win - HALF))],
                   axis=1)
    lk = jnp.arange(qb, dtype=i32)[None, :]
    step2 = jnp.concatenate(
        [jnp.concatenate([4 * (lk - i) + (jk - jq) for jk in range(4)], axis=1) for jq in range(4)], axis=0)
    b2 = tile(step2, 4, jnp.abs(step2) <= HALF)
    step3 = lk - i
    b3 = tile(step3, 16, jnp.abs(step3) <= HALF)
    return b1, b2, b3


def _softmax_parts(s):
    m = jnp.max(s, axis=-1, keepdims=True)
    p = jnp.exp(s - m)
    return m, p, jnp.sum(p, axis=-1, keepdims=True)


def _attn_kernel(q_ref, k_ref, v_ref, b1_ref, b2_ref, b3_ref, o_ref,
                 stage_ref, qp_ref, kp_ref, vp_ref, kpad_ref, vpad_ref, o23_ref, l23_ref):
    seq = q_ref.shape[0]
    qb = NAT_BLOCK
    nblk = seq // qb
    per_res = seq // RES16
    cls = 4 * per_res
    dn = (((1,), (1,)), ((), ()))

    def perm_row0(r16):
        return ((r16 % 4) * 4 + r16 // 4) * per_res

    for src, dst in ((q_ref, qp_ref), (k_ref, kp_ref), (v_ref, vp_ref)):
        stage_ref[...] = src[...].astype(f32)
        for r in range(RES16):
            dst[pl.ds(perm_row0(r), per_res), :] = stage_ref[pl.ds(r, per_res, stride=RES16), :].astype(bf16)

    zpad = jnp.zeros((HALF, HEAD_DIM), bf16)
    kpad_ref[pl.ds(0, HALF), :] = zpad
    vpad_ref[pl.ds(0, HALF), :] = zpad
    kpad_ref[pl.ds(HALF + seq, HALF), :] = zpad
    vpad_ref[pl.ds(HALF + seq, HALF), :] = zpad
    kpad_ref[pl.ds(HALF, seq), :] = k_ref[...]
    vpad_ref[pl.ds(HALF, seq), :] = v_ref[...]

    for r4 in range(4):
        base = r4 * cls
        q4 = qp_ref[pl.ds(base, cls), :]
        k4 = kp_ref[pl.ds(base, cls), :]
        v4 = vp_ref[pl.ds(base, cls), :]
        raw = lax.dot_general(q4, k4, dn, preferred_element_type=f32)
        m2, p2, d2 = _softmax_parts(raw + b2_ref[...])
        o2 = jnp.dot(p2.astype(bf16), v4, preferred_element_type=f32)
        m3s, d3s, o3s = [], [], []
        for jq in range(4):
            lo, hi = jq * per_res, (jq + 1) * per_res
            m3, p3, d3 = _softmax_parts(raw[lo:hi, lo:hi] + b3_ref[...])
            m3s.append(m3)
            d3s.append(d3)
            o3s.append(jnp.dot(p3.astype(bf16), v4[lo:hi], preferred_element_type=f32))
        m3 = jnp.concatenate(m3s, axis=0)
        d3 = jnp.concatenate(d3s, axis=0)
        o3 = jnp.concatenate(o3s, axis=0)
        m23 = jnp.maximum(m2, m3)
        w2 = jnp.exp(m2 - m23)
        w3 = jnp.exp(m3 - m23)
        d23 = w2 * d2 + w3 * d3
        o23_ref[pl.ds(base, cls), :] = (w2 * o2 + w3 * o3) / d23
        l23_ref[pl.ds(base, cls), :] = jnp.broadcast_to(m23 + jnp.log(d23), (cls, HEAD_DIM))

    for r in range(RES16):
        stage_ref[pl.ds(r, per_res, stride=RES16), :] = o23_ref[pl.ds(perm_row0(r), per_res), :]
    for r in range(RES16):
        o23_ref[pl.ds(r, per_res, stride=RES16), :] = l23_ref[pl.ds(perm_row0(r), per_res), :]

    def nat_block(n, carry):
        row0 = pl.multiple_of(n * qb, qb)
        q = q_ref[pl.ds(row0, qb), :]
        kw = kpad_ref[pl.ds(row0, qb + 2 * HALF), :]
        vw = vpad_ref[pl.ds(row0, qb + 2 * HALF), :]
        which = jnp.where(n == 0, 1, jnp.where(n == nblk - 1, 2, 0))
        s1 = lax.dot_general(q, kw, dn, preferred_element_type=f32) + b1_ref[which]
        m1, p1, d1 = _softmax_parts(s1)
        o1 = jnp.dot(p1.astype(bf16), vw, preferred_element_type=f32)
        l23 = o23_ref[pl.ds(row0, qb), :]
        on23 = stage_ref[pl.ds(row0, qb), :]
        mm = jnp.maximum(m1, l23)
        w1 = jnp.exp(m1 - mm)
        w23 = jnp.exp(l23 - mm)
        o_ref[pl.ds(row0, qb), :] = ((w1 * o1 + w23 * on23) / (w1 * d1 + w23)).astype(o_ref.dtype)
        return carry

    lax.fori_loop(0, nblk, nat_block, 0, unroll=2)


def _dilated_attention(proj3, b1, b2, b3):
    nb, seq, _ = proj3.shape
    assert seq // RES16 == Q_BLOCK and seq % NAT_BLOCK == 0
    q0 = 2 * D_LRU // HEAD_DIM
    k0 = q0 + N_HEADS
    v0 = k0 + N_HEADS
    return pl.pallas_call(
        _attn_kernel,
        out_shape=jax.ShapeDtypeStruct((nb, seq, D_ATTN), bf16),
        grid=(N_HEADS, nb),
        in_specs=[
            pl.BlockSpec((None, seq, HEAD_DIM), lambda h, b: (b, 0, q0 + h)),
            pl.BlockSpec((None, seq, HEAD_DIM), lambda h, b: (b, 0, k0 + h)),
            pl.BlockSpec((None, seq, HEAD_DIM), lambda h, b: (b, 0, v0 + h)),
            pl.BlockSpec((None, 3, NAT_BLOCK, NAT_BLOCK + 2 * HALF), lambda h, b: (h, 0, 0, 0)),
            pl.BlockSpec((None, 4 * Q_BLOCK, 4 * Q_BLOCK), lambda h, b: (h, 0, 0)),
            pl.BlockSpec((None, Q_BLOCK, Q_BLOCK), lambda h, b: (h, 0, 0)),
        ],
        out_specs=pl.BlockSpec((None, seq, HEAD_DIM), lambda h, b: (b, 0, h)),
        scratch_shapes=[
            pltpu.VMEM((seq, HEAD_DIM), f32),
            pltpu.VMEM((seq, HEAD_DIM), bf16),
            pltpu.VMEM((seq, HEAD_DIM), bf16),
            pltpu.VMEM((seq, HEAD_DIM), bf16),
            pltpu.VMEM((seq + 2 * HALF, HEAD_DIM), bf16),
            pltpu.VMEM((seq + 2 * HALF, HEAD_DIM), bf16),
            pltpu.VMEM((seq, HEAD_DIM), f32),
            pltpu.VMEM((seq, HEAD_DIM), f32),
        ],
        compiler_params=_cparams(("parallel", "parallel"), VMEM_LIMIT_MID),
        name="dilated_attn",
    )(proj3, proj3, proj3, b1, b2, b3)


def _layer_norm(z, g, b):
    mu = jnp.mean(z, axis=-1, keepdims=True)
    zc = z - mu
    var = jnp.mean(zc * zc, axis=-1, keepdims=True)
    return zc * lax.rsqrt(var + LN_EPS) * g + b


def _rms_norm(v, g):
    return v * lax.rsqrt(jnp.mean(v * v, axis=-1, keepdims=True) + LN_EPS) * g


def _pack_bf16_pairs(v):
    n = v.shape[1] // 2
    lo = pltpu.bitcast(v[:, :n].astype(bf16).astype(f32), u32)
    hi = pltpu.bitcast(v[:, n:].astype(bf16).astype(f32), u32)
    return (lo >> 16) | (hi & jnp.uint32(0xFFFF0000))


def _unpack_bf16_pairs(p):
    lo = pltpu.bitcast(p << 16, f32)
    hi = pltpu.bitcast(p & jnp.uint32(0xFFFF0000), f32)
    return jnp.concatenate([lo.astype(bf16), hi.astype(bf16)], axis=1)


LANES = 128
TOK_SUB = D_MODEL // 2 // LANES


def _store_token_tiles(ref, packed, row0=0):
    rows = packed.shape[0]
    for c in range(TOK_SUB):
        ref[pl.ds(TOK_SUB * row0 + c, rows, stride=TOK_SUB), :] = packed[:, c * LANES:(c + 1) * LANES]


def _load_token_tiles_bf16(ref, rows, row0=0):
    los, his = [], []
    for c in range(TOK_SUB):
        p = ref[pl.ds(TOK_SUB * row0 + c, rows, stride=TOK_SUB), :]
        los.append(pltpu.bitcast(p << 16, f32).astype(bf16))
        his.append(pltpu.bitcast(p & jnp.uint32(0xFFFF0000), f32).astype(bf16))
    return jnp.concatenate(los + his, axis=1)


def _outproj_kernel(lru_ref, att_ref, x_ref, gl_ref, ga_ref, w_ref, gm_ref, shf_ref, scf_ref,
                    lng_ref, lnb_ref, wrh_ref, wrl_ref, br_ref,
                    x1_ref, hp_ref, te_ref, tw_ref, rk_ref, cnt_ref, carry_ref):
    tm = x_ref.shape[0]
    step = pl.program_id(0)

    @pl.when(step == 0)
    def _():
        carry_ref[...] = jnp.zeros_like(carry_ref)

    ln = _rms_norm(lru_ref[...].astype(f32), gl_ref[...]).astype(bf16)
    an = _rms_norm(att_ref[...].astype(f32), ga_ref[...]).astype(bf16)
    y = (jnp.dot(ln, w_ref[pl.ds(0, D_LRU), :], preferred_element_type=f32)
         + jnp.dot(an, w_ref[pl.ds(D_LRU, D_ATTN), :], preferred_element_type=f32))
    x1 = _layer_norm(DEEPNORM_ALPHA * x_ref[...] + gm_ref[...] * y, lng_ref[...], lnb_ref[...])
    x1_ref[...] = x1
    hf = x1 * (1.0 + scf_ref[...]) + shf_ref[...]
    _store_token_tiles(hp_ref, _pack_bf16_pairs(hf))

    hf_hi = hf.astype(bf16)
    hf_lo = (hf - hf_hi.astype(f32)).astype(bf16)
    logits = (jnp.dot(hf_hi, wrh_ref[...], preferred_element_type=f32)
              + (jnp.dot(hf_lo, wrh_ref[...], preferred_element_type=f32)
                 + jnp.dot(hf_hi, wrl_ref[...], preferred_element_type=f32))
              + br_ref[...])
    lane = lax.broadcasted_iota(i32, (tm, N_EXPERTS), 1)
    col4 = lax.broadcasted_iota(i32, (tm, TOP_K), 1)
    work = logits
    vals, idxs = [], []
    for _k in range(TOP_K):
        m = jnp.max(work, axis=-1, keepdims=True)
        idx = jnp.min(jnp.where(work == m, lane, N_EXPERTS), axis=-1, keepdims=True)
        vals.append(m)
        idxs.append(idx)
        work = jnp.where(lane == idx, -jnp.inf, work)
    exps = [jnp.exp(v - vals[0]) for v in vals]
    den = exps[0] + exps[1] + exps[2] + exps[3]

    onehot = jnp.zeros((tm, N_EXPERTS), f32)
    for idx in idxs:
        onehot = onehot + (lane == idx).astype(f32)
    ri = lax.broadcasted_iota(i32, (tm, tm), 0)
    ci = lax.broadcasted_iota(i32, (tm, tm), 1)
    lower = (ri > ci).astype(bf16)
    before = jnp.dot(lower, onehot.astype(bf16), preferred_element_type=f32) + carry_ref[...]
    te = jnp.zeros((tm, TOP_K), i32)
    tw = jnp.zeros((tm, TOP_K), f32)
    rk = jnp.zeros((tm, TOP_K), i32)
    for k in range(TOP_K):
        rank_k = jnp.sum(jnp.where(lane == idxs[k], before, 0.0), axis=-1, keepdims=True)
        te = jnp.where(col4 == k, idxs[k], te)
        tw = jnp.where(col4 == k, exps[k] / den, tw)
        rk = jnp.where(col4 == k, rank_k.astype(i32), rk)
    te_ref[...] = te
    tw_ref[...] = tw
    rk_ref[...] = rk
    carry_ref[...] = carry_ref[...] + jnp.sum(onehot, axis=0, keepdims=True)
    cnt_ref[...] = carry_ref[...].astype(i32)


def _out_proj_router(lru2, att2, x2, g_lru, g_attn, w_out_bf, mod3, ln_g, ln_b, w_router, b_router, seq):
    t, d = x2.shape
    tm = TOK_TILE
    tps = seq // tm
    row = lambda i: (i, 0)
    const = lambda i: (0, 0)
    wr_hi = w_router.astype(bf16)
    wr_lo = (w_router - wr_hi.astype(f32)).astype(bf16)
    return pl.pallas_call(
        _outproj_kernel,
        out_shape=(
            jax.ShapeDtypeStruct((t, d), f32),
            jax.ShapeDtypeStruct((t * TOK_SUB, LANES), u32),
            jax.ShapeDtypeStruct((t, TOP_K), i32),
            jax.ShapeDtypeStruct((t, TOP_K), f32),
            jax.ShapeDtypeStruct((t, TOP_K), i32),
            jax.ShapeDtypeStruct((1, N_EXPERTS), i32),
        ),
        grid=(t // tm,),
        in_specs=[
            pl.BlockSpec((tm, D_LRU), row),
            pl.BlockSpec((tm, D_ATTN), row),
            pl.BlockSpec((tm, d), row),
            pl.BlockSpec((1, D_LRU), const),
            pl.BlockSpec((1, D_ATTN), const),
            pl.BlockSpec((d, d), const),
            pl.BlockSpec((None, 1, d), lambda i: (i // tps, 0, 2)),
            pl.BlockSpec((None, 1, d), lambda i: (i // tps, 0, 3)),
            pl.BlockSpec((None, 1, d), lambda i: (i // tps, 0, 4)),
            pl.BlockSpec((1, d), const),
            pl.BlockSpec((1, d), const),
            pl.BlockSpec((d, N_EXPERTS), const),
            pl.BlockSpec((d, N_EXPERTS), const),
            pl.BlockSpec((1, N_EXPERTS), const),
        ],
        out_specs=(
            pl.BlockSpec((tm, d), row),
            pl.BlockSpec((tm * TOK_SUB, LANES), row),
            pl.BlockSpec((tm, TOP_K), row),
            pl.BlockSpec((tm, TOP_K), row),
            pl.BlockSpec((tm, TOP_K), row),
            pl.BlockSpec((1, N_EXPERTS), const),
        ),
        scratch_shapes=[pltpu.VMEM((1, N_EXPERTS), f32)],
        compiler_params=_cparams(("arbitrary",), VMEM_LIMIT_BIG),
        name="out_proj_router",
    )(lru2, att2, x2, g_lru.reshape(1, -1), g_attn.reshape(1, -1), w_out_bf, mod3, mod3, mod3,
      ln_g.reshape(1, -1), ln_b.reshape(1, -1), wr_hi, wr_lo, b_router.reshape(1, -1))


def _dispatch_kernel(dest_ref, pad_start_ref, pad_len_ref, nt_ref, hp_ref, xs_ref, zero_ref, sem):
    tm = hp_ref.shape[0] // TOK_SUB
    tile_rows = EXP_TILE * TOK_SUB
    n_tiles = xs_ref.shape[0] // tile_rows

    def token_tile(ref, r):
        return ref.at[pl.ds(pl.multiple_of(r * TOK_SUB, TOK_SUB), TOK_SUB), :]

    def copy(tok, k):
        return pltpu.make_async_copy(token_tile(hp_ref, tok), token_tile(xs_ref, dest_ref[tok * TOP_K + k]), sem)

    def start(tok, c):
        for k in range(TOP_K):
            copy(tok, k).start(priority=k % 2)
        return c

    def wait(tok, c):
        for k in range(TOP_K):
            copy(tok, k).wait()
        return c

    lax.fori_loop(0, tm, start, 0, unroll=2)
    lax.fori_loop(0, tm, wait, 0, unroll=2)

    @pl.when(pl.program_id(0) == pl.num_programs(0) - 1)
    def _():
        zero_ref[...] = jnp.zeros_like(zero_ref)

        def per_expert(e, c):
            p0 = pad_start_ref[e]
            n = pad_len_ref[e]

            def zcopy(r):
                return pltpu.make_async_copy(token_tile(zero_ref, 0), token_tile(xs_ref, p0 + r), sem)

            lax.fori_loop(0, n, lambda r, c2: (zcopy(r).start(), c2)[1], 0)
            lax.fori_loop(0, n, lambda r, c2: (zcopy(r).wait(), c2)[1], 0)
            return c

        lax.fori_loop(0, N_EXPERTS, per_expert, 0)

        def tail_tile(m, c):
            cp = pltpu.make_async_copy(
                zero_ref, xs_ref.at[pl.ds(pl.multiple_of(m * tile_rows, tile_rows), tile_rows), :], sem)
            cp.start()
            cp.wait()
            return c

        lax.fori_loop(nt_ref[0], n_tiles, tail_tile, 0)


def _dispatch(hp, dest_flat, pad_start, pad_len, n_tiles_used, n_rows):
    t = hp.shape[0] // TOK_SUB
    tm = TOK_TILE
    smem = pl.BlockSpec(memory_space=pltpu.SMEM)
    return pl.pallas_call(
        _dispatch_kernel,
        out_shape=jax.ShapeDtypeStruct((n_rows * TOK_SUB, LANES), u32),
        grid=(t // tm,),
        in_specs=[
            pl.BlockSpec((tm * TOP_K,), lambda i: (i,), memory_space=pltpu.SMEM),
            smem, smem, smem,
            pl.BlockSpec((tm * TOK_SUB, LANES), lambda i: (i, 0)),
        ],
        out_specs=pl.BlockSpec(memory_space=pl.ANY),
        scratch_shapes=[pltpu.VMEM((EXP_TILE * TOK_SUB, LANES), u32), pltpu.SemaphoreType.DMA(())],
        compiler_params=_cparams(("arbitrary",), VMEM_LIMIT_MID),
        name="moe_dispatch",
    )(dest_flat, pad_start, pad_len, n_tiles_used, hp)


def _group_start(te_ref, m):
    return jnp.logical_or(m == 0, te_ref[m] != te_ref[jnp.maximum(m - 1, 0)])


def _gate_up_kernel(te_ref, nx_ref, rv_ref, nt_ref, xs_ref, w_hbm, bg_ref, bu_ref, h_ref,
                    wbuf, w_scr, sems, grp_ref, *, layer):
    j = pl.program_id(0)
    m = pl.program_id(1)
    n_pass = pl.num_programs(0)
    fc = w_scr.shape[1] // 2
    half = EXP_TILE // 2
    valid = m < nt_ref[0]

    def copies(e, jj, slot):
        col = pl.multiple_of(jj * fc, fc)
        return (pltpu.make_async_copy(w_hbm.at[layer, e, :, pl.ds(col, fc)], wbuf.at[slot, 0], sems.at[slot, 0]),
                pltpu.make_async_copy(w_hbm.at[layer, e, :, pl.ds(pl.multiple_of(D_FF + col, fc), fc)],
                                      wbuf.at[slot, 1], sems.at[slot, 1]))

    @pl.when(jnp.logical_and(_group_start(te_ref, m), valid))
    def _():
        @pl.when(jnp.logical_and(j == 0, m == 0))
        def _():
            grp_ref[0] = 0
            for c in copies(te_ref[0], 0, 0):
                c.start()

        g = grp_ref[0]
        slot = g & 1
        for c in copies(te_ref[m], j, slot):
            c.wait()
        nxt = nx_ref[m]

        @pl.when(nxt >= 0)
        def _():
            for c in copies(nxt, j, 1 - slot):
                c.start()

        @pl.when(jnp.logical_and(nxt < 0, j + 1 < n_pass))
        def _():
            for c in copies(te_ref[0], j + 1, 1 - slot):
                c.start()

        w_scr[:, pl.ds(0, fc)] = wbuf[slot, 0].astype(bf16)
        w_scr[:, pl.ds(fc, fc)] = wbuf[slot, 1].astype(bf16)
        grp_ref[0] = g + 1

    def swiglu(x):
        gu = jnp.dot(x, w_scr[...], preferred_element_type=f32)
        gate = jnp.minimum(gu[:, :fc] + bg_ref[...], SWIGLU_LIMIT)
        up = jnp.clip(gu[:, fc:] + bu_ref[...], -SWIGLU_LIMIT, SWIGLU_LIMIT)
        glu = gate * jax.nn.sigmoid(gate * SWIGLU_ALPHA)
        return _pack_bf16_pairs((up + 1.0) * glu)

    @pl.when(rv_ref[m] > half)
    def _():
        h_ref[...] = swiglu(_load_token_tiles_bf16(xs_ref, EXP_TILE))

    @pl.when(jnp.logical_and(rv_ref[m] > 0, rv_ref[m] <= half))
    def _():
        h_ref[pl.ds(0, half), :] = swiglu(_load_token_tiles_bf16(xs_ref, half))
        h_ref[pl.ds(half, half), :] = jnp.zeros((half, h_ref.shape[1]), h_ref.dtype)

    @pl.when(rv_ref[m] == 0)
    def _():
        h_ref[...] = jnp.zeros_like(h_ref)


def _gate_up(xs, w_gate_up, b_gate_up4, layer, tile_e, next_e, rows_valid, n_tiles_used):
    p = xs.shape[0] // TOK_SUB
    d = D_MODEL
    fc = D_FF // 2
    n_pass = D_FF // fc
    tm = EXP_TILE
    grid_spec = pltpu.PrefetchScalarGridSpec(
        num_scalar_prefetch=4,
        grid=(n_pass, p // tm),
        in_specs=[
            pl.BlockSpec((tm * TOK_SUB, LANES), lambda j, m, te, nx, rv, nt: (m, 0)),
            pl.BlockSpec(memory_space=pl.ANY),
            pl.BlockSpec((None, None, 1, fc), lambda j, m, te, nx, rv, nt: (layer, te[m], 0, j)),
            pl.BlockSpec((None, None, 1, fc), lambda j, m, te, nx, rv, nt: (layer, te[m], 0, n_pass + j)),
        ],
        out_specs=pl.BlockSpec((tm, fc // 2), lambda j, m, te, nx, rv, nt: (m, j)),
        scratch_shapes=[
            pltpu.VMEM((2, 2, d, fc), f32),
            pltpu.VMEM((d, 2 * fc), bf16),
            pltpu.SemaphoreType.DMA((2, 2)),
            pltpu.SMEM((1,), i32),
        ],
    )
    return pl.pallas_call(
        functools.partial(_gate_up_kernel, layer=layer),
        out_shape=jax.ShapeDtypeStruct((p, D_FF // 2), u32),
        grid_spec=grid_spec,
        compiler_params=_cparams(("arbitrary", "arbitrary"), VMEM_LIMIT_BIG),
        name="moe_gate_up",
    )(tile_e, next_e, rows_valid, n_tiles_used, xs, w_gate_up, b_gate_up4, b_gate_up4)


def _down_kernel(te_ref, nx_ref, rv_ref, nt_ref, h_ref, w_hbm, b_ref, y_ref, wbuf, w_scr, sems, grp_ref, *, layer):
    m = pl.program_id(0)
    fc = D_FF // 2
    half = EXP_TILE // 2
    valid = m < nt_ref[0]

    def copy(e, slot):
        return pltpu.make_async_copy(w_hbm.at[layer, e], wbuf.at[slot], sems.at[slot])

    @pl.when(jnp.logical_and(_group_start(te_ref, m), valid))
    def _():
        @pl.when(m == 0)
        def _():
            grp_ref[0] = 0
            copy(te_ref[0], 0).start()

        g = grp_ref[0]
        slot = g & 1
        copy(te_ref[m], slot).wait()
        nxt = nx_ref[m]

        @pl.when(nxt >= 0)
        def _():
            copy(nxt, 1 - slot).start()

        w_scr[...] = wbuf[slot].astype(bf16)
        grp_ref[0] = g + 1

    def project(hp):
        h = jnp.concatenate([_unpack_bf16_pairs(hp[:, :fc // 2]), _unpack_bf16_pairs(hp[:, fc // 2:])], axis=1)
        return _pack_bf16_pairs(jnp.dot(h, w_scr[...], preferred_element_type=f32) + b_ref[...])

    @pl.when(rv_ref[m] > half)
    def _():
        _store_token_tiles(y_ref, project(h_ref[...]))

    @pl.when(jnp.logical_and(rv_ref[m] > 0, rv_ref[m] <= half))
    def _():
        _store_token_tiles(y_ref, project(h_ref[pl.ds(0, half), :]))
        y_ref[pl.ds(half * TOK_SUB, half * TOK_SUB), :] = jnp.zeros((half * TOK_SUB, LANES), y_ref.dtype)

    @pl.when(rv_ref[m] == 0)
    def _():
        y_ref[...] = jnp.zeros_like(y_ref)


def _down(hs, w_down, b_down4, layer, tile_e, next_e, rows_valid, n_tiles_used):
    p, half = hs.shape
    d_ff = 2 * half
    d = w_down.shape[3]
    tm = EXP_TILE
    grid_spec = pltpu.PrefetchScalarGridSpec(
        num_scalar_prefetch=4,
        grid=(p // tm,),
        in_specs=[
            pl.BlockSpec((tm, half), lambda m, te, nx, rv, nt: (m, 0)),
            pl.BlockSpec(memory_space=pl.ANY),
            pl.BlockSpec((None, None, 1, d), lambda m, te, nx, rv, nt: (layer, te[m], 0, 0)),
        ],
        out_specs=pl.BlockSpec((tm * TOK_SUB, LANES), lambda m, te, nx, rv, nt: (m, 0)),
        scratch_shapes=[
            pltpu.VMEM((2, d_ff, d), f32),
            pltpu.VMEM((d_ff, d), bf16),
            pltpu.SemaphoreType.DMA((2,)),
            pltpu.SMEM((1,), i32),
        ],
    )
    return pl.pallas_call(
        functools.partial(_down_kernel, layer=layer),
        out_shape=jax.ShapeDtypeStruct((p * TOK_SUB, LANES), u32),
        grid_spec=grid_spec,
        compiler_params=_cparams(("arbitrary",), VMEM_LIMIT_BIG),
        name="moe_down",
    )(tile_e, next_e, rows_valid, n_tiles_used, hs, w_down, b_down4)


def _combine_kernel(dest_ref, ys_ref, tw_ref, x1_ref, gf_ref, lng_ref, lnb_ref, o_ref, buf_ref, sem):
    tm = x1_ref.shape[0]

    def token_tile(ref, r):
        return ref.at[pl.ds(pl.multiple_of(r * TOK_SUB, TOK_SUB), TOK_SUB), :]

    def copy(tok, k):
        return pltpu.make_async_copy(token_tile(ys_ref, dest_ref[tok * TOP_K + k]),
                                     token_tile(buf_ref, k * tm + tok), sem)

    def start(tok, c):
        for k in range(TOP_K):
            copy(tok, k).start(priority=k % 2)
        return c

    def wait(tok, c):
        for k in range(TOP_K):
            copy(tok, k).wait()
        return c

    lax.fori_loop(0, tm, start, 0, unroll=2)
    lax.fori_loop(0, tm, wait, 0, unroll=2)

    tw = tw_ref[...]
    y = jnp.zeros((tm, D_MODEL), f32)
    for k in range(TOP_K):
        y = y + tw[:, k:k + 1] * _load_token_tiles_bf16(buf_ref, tm, row0=k * tm).astype(f32)
    o_ref[...] = _layer_norm(DEEPNORM_ALPHA * x1_ref[...] + gf_ref[...] * y, lng_ref[...], lnb_ref[...])


def _combine(ys, dest_flat, top_w, x1, mod3, ln_g, ln_b, seq):
    t, d = x1.shape
    tm = TOK_TILE
    tps = seq // tm
    grid_spec = pltpu.PrefetchScalarGridSpec(
        num_scalar_prefetch=0,
        grid=(t // tm,),
        in_specs=[
            pl.BlockSpec((tm * TOP_K,), lambda i: (i,), memory_space=pltpu.SMEM),
            pl.BlockSpec(memory_space=pl.ANY),
            pl.BlockSpec((tm, TOP_K), lambda i: (i, 0)),
            pl.BlockSpec((tm, d), lambda i: (i, 0)),
            pl.BlockSpec((None, 1, d), lambda i: (i // tps, 0, 5)),
            pl.BlockSpec((1, d), lambda i: (0, 0)),
            pl.BlockSpec((1, d), lambda i: (0, 0)),
        ],
        out_specs=pl.BlockSpec((tm, d), lambda i: (i, 0)),
        scratch_shapes=[pltpu.VMEM((TOP_K * tm * TOK_SUB, LANES), u32), pltpu.SemaphoreType.DMA(())],
    )
    return pl.pallas_call(
        _combine_kernel,
        out_shape=jax.ShapeDtypeStruct((t, d), f32),
        grid_spec=grid_spec,
        compiler_params=_cparams(("arbitrary",), VMEM_LIMIT_MID),
        name="moe_combine",
    )(dest_flat, ys, top_w, x1, mod3, ln_g.reshape(1, -1), ln_b.reshape(1, -1))


def _routing_tables(top_e, rank, counts, n_tiles):
    tm = EXP_TILE
    counts = counts.reshape(-1)
    padded = (counts + tm - 1) // tm * tm
    pend = jnp.cumsum(padded)
    pstart = pend - padded
    dest = (pstart[top_e] + rank).reshape(-1).astype(i32)
    tile_start = jnp.arange(n_tiles, dtype=i32) * tm
    tile_e = jnp.sum((pend[None, :] <= tile_start[:, None]).astype(i32), axis=1)
    tile_e = jnp.minimum(tile_e, N_EXPERTS - 1).astype(i32)
    n_used = (pend[-1] // tm).astype(i32).reshape(1)
    mine = jnp.arange(N_EXPERTS, dtype=i32)[None, :] == tile_e[:, None]
    group_end = jnp.sum(jnp.where(mine, pend[None, :], 0), axis=1)
    data_end = jnp.sum(jnp.where(mine, (pstart + counts)[None, :], 0), axis=1)
    rows_valid = jnp.where(tile_start < pend[-1], jnp.clip(data_end - tile_start, 0, tm), 0).astype(i32)
    next_e = jnp.sum((pend[None, :] <= group_end[:, None]).astype(i32), axis=1)
    next_e = jnp.where(group_end < pend[-1], jnp.minimum(next_e, N_EXPERTS - 1), -1).astype(i32)
    return dest, tile_e, next_e, rows_valid, n_used, (pstart + counts).astype(i32), (padded - counts).astype(i32)


def kernel(x, c, rel_bias, w_ada, b_ada, w_in, conv_w, conv_b, lru_w_a, lru_b_a, lru_w_x, lru_b_x, lru_lambda, norm_lru_g, norm_attn_g, w_out, ln_mix_g, ln_mix_b, w_router, b_router, w_gate_up, b_gate_up, w_down, b_down, ln_ffn_g, ln_ffn_b):
    nb, seq, d = x.shape
    t = nb * seq
    mod = _ada_mod(c, w_ada, b_ada)
    b1, b2, b3 = _attn_bias_tables(rel_bias)
    n_tiles = t * TOP_K // EXP_TILE + N_EXPERTS
    q_lo = 2 * D_LRU
    x2 = x.reshape(t, d)
    for l in range(DEPTH):
        mod3 = mod[l].reshape(nb, 1, 6 * d)
        w_in_l = w_in[l].at[:, q_lo:q_lo + D_ATTN].multiply(HEAD_DIM ** -0.5).astype(bf16)
        proj = _in_proj(x2, mod3, w_in_l, seq).reshape(nb, seq, D_IN)
        wg = jnp.concatenate([lru_w_a[l], lru_w_x[l]], axis=-1).astype(bf16)
        bg = jnp.concatenate([lru_b_a[l].reshape(2, N_LRU_BLOCKS, 1, LRU_BLOCK),
                              lru_b_x[l].reshape(2, N_LRU_BLOCKS, 1, LRU_BLOCK)], axis=-1)
        lru = _lru_group(proj, conv_w[l], conv_b[l], wg, bg, lru_lambda[l])
        att = _dilated_attention(proj, b1, b2, b3)
        x1, hp, top_e, top_w, rank, counts = _out_proj_router(
            lru.reshape(t, D_LRU), att.reshape(t, D_ATTN), x2, norm_lru_g[l], norm_attn_g[l],
            w_out[l].astype(bf16), mod3, ln_mix_g[l], ln_mix_b[l], w_router[l], b_router[l], seq)
        dest, tile_e, next_e, rows_valid, n_used, pad_start, pad_len = _routing_tables(top_e, rank, counts, n_tiles)
        xs = _dispatch(hp, dest, pad_start, pad_len, n_used, n_tiles * EXP_TILE)
        hs = _gate_up(xs, w_gate_up, b_gate_up.reshape(DEPTH, N_EXPERTS, 1, 2 * D_FF), l,
                      tile_e, next_e, rows_valid, n_used)
        ys = _down(hs, w_down, b_down.reshape(DEPTH, N_EXPERTS, 1, d), l, tile_e, next_e, rows_valid, n_used)
        x2 = _combine(ys, dest, top_w, x1, mod3, ln_ffn_g[l], ln_ffn_b[l], seq)
    return x2.reshape(nb, seq, d)
```

```python
import functools
import math

import jax
import jax.numpy as jnp
from jax import lax
from jax.experimental import pallas as pl
from jax.experimental.pallas import tpu as pltpu

f32 = jnp.float32
bf16 = jnp.bfloat16
i32 = jnp.int32
u32 = jnp.uint32

D_MODEL = 2048
DEPTH = 2
D_LRU = 1024
N_LRU_BLOCKS = 8
LRU_BLOCK = 128
CONV_WIDTH = 4
LRU_C = 8.0
D_ATTN = 1024
HEAD_DIM = 128
N_HEADS = 8
D_IN = 2 * D_LRU + 3 * D_ATTN
NEG_INF = -1e30
N_BUCKETS = 32
MAX_EXACT = 8
MAX_DISTANCE = 1024
N_EXPERTS = 32
TOP_K = 4
D_FF = D_MODEL
SWIGLU_LIMIT = 7.0
SWIGLU_ALPHA = 1.702
DEEPNORM_ALPHA = (2 * DEPTH) ** 0.25
LN_EPS = 1e-5

VMEM_LIMIT_BIG = 56 * 1024 * 1024
VMEM_LIMIT_MID = 40 * 1024 * 1024

ROW_TILE = 512
ROUTER_TILE = 512
TOK_TILE = 256
EXP_TILE = 512
LRU_CHUNK = 128
Q_BLOCK = 128
NAT_BLOCK = 256
SUB_ROWS = 128
HALF = 64
RES16 = 16
STAGE_PITCH = 24


def _cparams(sem, vmem=None):
    return pltpu.CompilerParams(dimension_semantics=sem, vmem_limit_bytes=vmem)


def _ada_kernel(c_ref, w_ref, b_ref, o_ref):
    c = c_ref[...]
    ca = c * jax.nn.sigmoid(c)
    o_ref[0] = jnp.dot(ca, w_ref[0], precision=lax.Precision.HIGHEST,
                       preferred_element_type=f32) + b_ref[0]


def _ada_mod(c, w_ada, b_ada):
    depth, d, n = w_ada.shape
    b = c.shape[0]
    tn = 1024
    return pl.pallas_call(
        _ada_kernel,
        out_shape=jax.ShapeDtypeStruct((depth, b, n), f32),
        grid=(depth, n // tn),
        in_specs=[
            pl.BlockSpec((b, d), lambda l, j: (0, 0)),
            pl.BlockSpec((1, d, tn), lambda l, j: (l, 0, j)),
            pl.BlockSpec((1, 1, tn), lambda l, j: (l, 0, j)),
        ],
        out_specs=pl.BlockSpec((1, b, tn), lambda l, j: (l, 0, j)),
        compiler_params=_cparams(("parallel", "parallel"), VMEM_LIMIT_MID),
        name="ada_mod",
    )(c, w_ada, b_ada.reshape(depth, 1, n))


def _inproj_kernel(x_ref, sh_ref, sc_ref, w_ref, o_ref):
    h = x_ref[...] * (1.0 + sc_ref[...]) + sh_ref[...]
    o_ref[...] = jnp.dot(h.astype(bf16), w_ref[...], preferred_element_type=f32).astype(o_ref.dtype)


def _in_proj(x2, mod3, w_in_bf, seq):
    t, d = x2.shape
    n = w_in_bf.shape[1]
    tn = n // 2
    tiles_per_seq = seq // ROW_TILE
    return pl.pallas_call(
        _inproj_kernel,
        out_shape=jax.ShapeDtypeStruct((t, n), bf16),
        grid=(n // tn, t // ROW_TILE),
        in_specs=[
            pl.BlockSpec((ROW_TILE, d), lambda j, i: (i, 0)),
            pl.BlockSpec((None, 1, d), lambda j, i: (i // tiles_per_seq, 0, 0)),
            pl.BlockSpec((None, 1, d), lambda j, i: (i // tiles_per_seq, 0, 1)),
            pl.BlockSpec((d, tn), lambda j, i: (0, j)),
        ],
        out_specs=pl.BlockSpec((ROW_TILE, tn), lambda j, i: (i, j)),
        compiler_params=_cparams(("parallel", "parallel"), VMEM_LIMIT_BIG),
        name="in_proj",
    )(x2, mod3, mod3, w_in_bf)


def _lru_kernel(xb_ref, gb_ref, cw_ref, cb_ref, wg_ref, bg_ref, lam_ref, o_ref,
                xt_ref, hs_ref, a_ref, b_ref):
    nb, seq, _ = xb_ref.shape
    tc = LRU_CHUNK
    rows = tc * nb
    n_chunks = seq // tc
    front = (CONV_WIDTH // 2) * nb

    xt_ref[pl.ds(0, front), :] = jnp.zeros((front, LRU_BLOCK), f32)
    xt_ref[pl.ds(front + seq * nb, nb), :] = jnp.zeros((nb, LRU_BLOCK), f32)
    for b in range(nb):
        xt_ref[pl.ds(front + b, seq, stride=nb), :] = xb_ref[b].astype(f32)

    lam = lam_ref[...]
    neg_c_sp = -LRU_C * jax.nn.softplus(-lam)
    cw = cw_ref[...]
    cb = cb_ref[...]

    def gates(d, r0):
        xc = cb
        for j in range(CONV_WIDTH):
            xc = xc + cw[j:j + 1, :] * xt_ref[pl.ds(pl.multiple_of(r0 + j * nb, nb), rows), :]
        g = jnp.dot(xc.astype(bf16), wg_ref[d], preferred_element_type=f32) + bg_ref[d]
        r = 0.5 + 0.5 * jnp.tanh(0.5 * g[:, :LRU_BLOCK])
        i = 0.5 + 0.5 * jnp.tanh(0.5 * g[:, LRU_BLOCK:])
        log_a = neg_c_sp[d:d + 1, :] * r
        a = jnp.exp(log_a)
        u = a * a
        z = 2.0 * log_a
        one_minus = jnp.where(u == 1.0, -z, (1.0 - u) * z / jnp.log(u))
        root = jnp.where(one_minus > 0.0, one_minus * lax.rsqrt(one_minus), 0.0)
        a_ref[...] = a
        b_ref[...] = root * (i * xc)

    def fwd_chunk(c, h):
        r0 = pl.multiple_of(c * rows, rows)
        gates(0, r0)

        def step(t, h):
            rr = pl.multiple_of(t * nb, nb)
            h = a_ref[pl.ds(rr, nb), :] * h + b_ref[pl.ds(rr, nb), :]
            hs_ref[pl.ds(r0 + rr, nb), :] = h
            return h

        return lax.fori_loop(0, tc, step, h, unroll=8)

    lax.fori_loop(0, n_chunks, fwd_chunk, jnp.zeros((nb, LRU_BLOCK), f32))

    def bwd_chunk(ci, h):
        c = n_chunks - 1 - ci
        r0 = pl.multiple_of(c * rows, rows)
        gates(1, r0)

        def step(ti, h):
            rr = pl.multiple_of((tc - 1 - ti) * nb, nb)
            h = a_ref[pl.ds(rr, nb), :] * h + b_ref[pl.ds(rr, nb), :]
            b_ref[pl.ds(rr, nb), :] = h
            return h

        h = lax.fori_loop(0, tc, step, h, unroll=8)
        hs_ref[pl.ds(r0, rows), :] = hs_ref[pl.ds(r0, rows), :] + b_ref[...]
        t0 = pl.multiple_of(c * tc, tc)
        for b in range(nb):
            hr = hs_ref[pl.ds(r0 + b, tc, stride=nb), :]
            gate = jax.nn.gelu(gb_ref[b, pl.ds(t0, tc), :].astype(f32))
            o_ref[b, pl.ds(t0, tc), :] = (hr * gate).astype(o_ref.dtype)
        return h

    lax.fori_loop(0, n_chunks, bwd_chunk, jnp.zeros((nb, LRU_BLOCK), f32))


def _lru_group(proj3, conv_w, conv_b, wg, bg, lam):
    nb, seq, _ = proj3.shape
    g = N_LRU_BLOCKS
    return pl.pallas_call(
        _lru_kernel,
        out_shape=jax.ShapeDtypeStruct((nb, seq, D_LRU), bf16),
        grid=(g,),
        in_specs=[
            pl.BlockSpec((nb, seq, LRU_BLOCK), lambda j: (0, 0, j)),
            pl.BlockSpec((nb, seq, LRU_BLOCK), lambda j: (0, 0, g + j)),
            pl.BlockSpec((CONV_WIDTH, LRU_BLOCK), lambda j: (0, j)),
            pl.BlockSpec((1, LRU_BLOCK), lambda j: (0, j)),
            pl.BlockSpec((2, None, LRU_BLOCK, 2 * LRU_BLOCK), lambda j: (0, j, 0, 0)),
            pl.BlockSpec((2, None, 1, 2 * LRU_BLOCK), lambda j: (0, j, 0, 0)),
            pl.BlockSpec((2, LRU_BLOCK), lambda j: (0, j)),
        ],
        out_specs=pl.BlockSpec((nb, seq, LRU_BLOCK), lambda j: (0, 0, j)),
        scratch_shapes=[
            pltpu.VMEM(((seq + CONV_WIDTH - 1) * nb, LRU_BLOCK), f32),
            pltpu.VMEM((seq * nb, LRU_BLOCK), f32),
            pltpu.VMEM((LRU_CHUNK * nb, LRU_BLOCK), f32),
            pltpu.VMEM((LRU_CHUNK * nb, LRU_BLOCK), f32),
        ],
        compiler_params=_cparams(("parallel",), VMEM_LIMIT_BIG),
        name="rg_lru",
    )(proj3, proj3, conv_w, conv_b.reshape(1, D_LRU), wg, bg, lam)


def _t5_bucket(rel):
    nbk = N_BUCKETS // 2
    ret = jnp.where(rel > 0, nbk, 0)
    n = jnp.abs(rel)
    nf = jnp.maximum(n, 1).astype(f32)
    large = MAX_EXACT + (jnp.log(nf / MAX_EXACT) / math.log(MAX_DISTANCE / MAX_EXACT)
                         * (nbk - MAX_EXACT)).astype(i32)
    large = jnp.minimum(large, nbk - 1)
    return ret + jnp.where(n < MAX_EXACT, n, large)


def _attn_bias_tables(rel_bias):
    qb = Q_BLOCK
    rb = rel_bias.astype(f32)
    i = jnp.arange(qb, dtype=i32)[:, None]

    def tile(step, dilation, valid):
        onehot = jax.nn.one_hot(_t5_bucket(step * dilation), N_BUCKETS, dtype=f32)
        t = jnp.einsum("qkn,nh->hqk", onehot, rb, precision=lax.Precision.HIGHEST)
        return jnp.where(valid[None], t, NEG_INF)

    i1 = jnp.arange(NAT_BLOCK, dtype=i32)[:, None]
    win = NAT_BLOCK + 2 * HALF
    j = jnp.arange(win, dtype=i32)[None, :]
    step1 = j - HALF - i1
    band1 = jnp.abs(step1) <= HALF
    b1 = jnp.stack([tile(step1, 1, band1),
                    tile(step1, 1, band1 & (j >= HALF)),
                    tile(step1, 1, band1 & (j < win - HALF))],
                   axis=1)
    lk = jnp.arange(qb, dtype=i32)[None, :]
    step2 = jnp.concatenate(
        [jnp.concatenate([4 * (lk - i) + (jk - jq) for jk in range(4)], axis=1) for jq in range(4)], axis=0)
    b2 = tile(step2, 4, jnp.abs(step2) <= HALF)
    step3 = lk - i
    b3 = tile(step3, 16, jnp.abs(step3) <= HALF)
    return b1, b2, b3


def _lane_tiles(a):
    return [a[:, c:c + LANES] for c in range(0, a.shape[1], LANES)] if a.shape[1] % LANES == 0 else [a]


def _softmax_parts(s):
    m = jnp.max(functools.reduce(jnp.maximum, _lane_tiles(s)), axis=-1, keepdims=True)
    p = jnp.exp(s - m)
    return m, p, jnp.sum(functools.reduce(jnp.add, _lane_tiles(p)), axis=-1, keepdims=True)


def _attn_kernel(q_ref, k_ref, v_ref, b1_ref, b2_ref, b3_ref, o_ref,
                 stage_ref, qp_ref, kp_ref, vp_ref, kpad_ref, vpad_ref, o23_ref, l23_ref):
    seq = q_ref.shape[0]
    qb = NAT_BLOCK
    nblk = seq // qb
    per_res = seq // RES16
    cls = 4 * per_res
    dn = (((1,), (1,)), ((), ()))

    def perm_row0(r16):
        return ((r16 % 4) * 4 + r16 // 4) * per_res

    def pitched(ref, g, rows=RES16):
        return ref.at[pl.ds(pl.multiple_of(g * STAGE_PITCH, 8), rows), :]

    for src, dst in ((q_ref, qp_ref), (k_ref, kp_ref), (v_ref, vp_ref)):
        def fill(g, c, src=src):
            pitched(stage_ref, g)[...] = src[pl.ds(pl.multiple_of(g * RES16, RES16), RES16), :].astype(f32)
            return c

        lax.fori_loop(0, per_res, fill, 0, unroll=8)
        for r in range(RES16):
            dst[pl.ds(perm_row0(r), per_res), :] = stage_ref[pl.ds(r, per_res, stride=STAGE_PITCH), :].astype(bf16)

    zpad = jnp.zeros((HALF, HEAD_DIM), bf16)
    kpad_ref[pl.ds(0, HALF), :] = zpad
    vpad_ref[pl.ds(0, HALF), :] = zpad
    kpad_ref[pl.ds(HALF + seq, HALF), :] = zpad
    vpad_ref[pl.ds(HALF + seq, HALF), :] = zpad
    kpad_ref[pl.ds(HALF, seq), :] = k_ref[...]
    vpad_ref[pl.ds(HALF, seq), :] = v_ref[...]

    for r4 in range(4):
        base = r4 * cls
        q4 = qp_ref[pl.ds(base, cls), :]
        k4 = kp_ref[pl.ds(base, cls), :]
        v4 = vp_ref[pl.ds(base, cls), :]
        raw = lax.dot_general(q4, k4, dn, preferred_element_type=f32)
        m2, p2, d2 = _softmax_parts(raw + b2_ref[...])
        o2 = jnp.dot(p2.astype(bf16), v4, preferred_element_type=f32)
        m3s, d3s, o3s = [], [], []
        for jq in range(4):
            lo, hi = jq * per_res, (jq + 1) * per_res
            m3, p3, d3 = _softmax_parts(raw[lo:hi, lo:hi] + b3_ref[...])
            m3s.append(m3)
            d3s.append(d3)
            o3s.append(jnp.dot(p3.astype(bf16), v4[lo:hi], preferred_element_type=f32))
        m3 = jnp.concatenate(m3s, axis=0)
        d3 = jnp.concatenate(d3s, axis=0)
        o3 = jnp.concatenate(o3s, axis=0)
        m23 = jnp.maximum(m2, m3)
        w2 = jnp.exp(m2 - m23)
        w3 = jnp.exp(m3 - m23)
        d23 = w2 * d2 + w3 * d3
        o23_ref[pl.ds(base, cls), :] = (w2 * o2 + w3 * o3) / d23
        l23_ref[pl.ds(base, cls), :] = jnp.broadcast_to(m23 + jnp.log(d23), (cls, HEAD_DIM))

    for r in range(RES16):
        stage_ref[pl.ds(r, per_res, stride=STAGE_PITCH), :] = o23_ref[pl.ds(perm_row0(r), per_res), :]
    for r in range(RES16):
        o23_ref[pl.ds(r, per_res, stride=STAGE_PITCH), :] = l23_ref[pl.ds(perm_row0(r), per_res), :]

    def natural_rows(ref, g0, rows):
        return jnp.concatenate([pitched(ref, g0 + i)[...] for i in range(rows // RES16)], axis=0)

    per_group = 2
    n_groups = nblk // per_group

    def nat_group(gi, carry):
        for sb in range(per_group):
            row0 = pl.multiple_of((gi * per_group + sb) * qb, qb)
            q = q_ref[pl.ds(row0, qb), :]
            kw = kpad_ref[pl.ds(row0, qb + 2 * HALF), :]
            vw = vpad_ref[pl.ds(row0, qb + 2 * HALF), :]
            if sb == 0:
                bias = b1_ref[jnp.where(gi == 0, 1, 0)]
            elif sb == per_group - 1:
                bias = b1_ref[jnp.where(gi == n_groups - 1, 2, 0)]
            else:
                bias = b1_ref[0]
            s1 = lax.dot_general(q, kw, dn, preferred_element_type=f32) + bias
            m1, p1, d1 = _softmax_parts(s1)
            o1 = jnp.dot(p1.astype(bf16), vw, preferred_element_type=f32)
            g0 = (gi * per_group + sb) * (qb // RES16)
            l23 = natural_rows(o23_ref, g0, qb)
            on23 = natural_rows(stage_ref, g0, qb)
            mm = jnp.maximum(m1, l23)
            w1 = jnp.exp(m1 - mm)
            w23 = jnp.exp(l23 - mm)
            o_ref[pl.ds(row0, qb), :] = ((w1 * o1 + w23 * on23) / (w1 * d1 + w23)).astype(o_ref.dtype)
        return carry

    lax.fori_loop(0, n_groups, nat_group, 0)


def _dilated_attention(proj3, b1, b2, b3):
    nb, seq, _ = proj3.shape
    assert seq // RES16 == Q_BLOCK and seq % (2 * NAT_BLOCK) == 0 and Q_BLOCK % SUB_ROWS == 0
    q0 = 2 * D_LRU // HEAD_DIM
    k0 = q0 + N_HEADS
    v0 = k0 + N_HEADS
    return pl.pallas_call(
        _attn_kernel,
        out_shape=jax.ShapeDtypeStruct((nb, seq, D_ATTN), bf16),
        grid=(N_HEADS, nb),
        in_specs=[
            pl.BlockSpec((None, seq, HEAD_DIM), lambda h, b: (b, 0, q0 + h)),
            pl.BlockSpec((None, seq, HEAD_DIM), lambda h, b: (b, 0, k0 + h)),
            pl.BlockSpec((None, seq, HEAD_DIM), lambda h, b: (b, 0, v0 + h)),
            pl.BlockSpec((None, 3, NAT_BLOCK, NAT_BLOCK + 2 * HALF), lambda h, b: (h, 0, 0, 0)),
            pl.BlockSpec((None, 4 * Q_BLOCK, 4 * Q_BLOCK), lambda h, b: (h, 0, 0)),
            pl.BlockSpec((None, Q_BLOCK, Q_BLOCK), lambda h, b: (h, 0, 0)),
        ],
        out_specs=pl.BlockSpec((None, seq, HEAD_DIM), lambda h, b: (b, 0, h)),
        scratch_shapes=[
            pltpu.VMEM((seq // RES16 * STAGE_PITCH, HEAD_DIM), f32),
            pltpu.VMEM((seq, HEAD_DIM), bf16),
            pltpu.VMEM((seq, HEAD_DIM), bf16),
            pltpu.VMEM((seq, HEAD_DIM), bf16),
            pltpu.VMEM((seq + 2 * HALF, HEAD_DIM), bf16),
            pltpu.VMEM((seq + 2 * HALF, HEAD_DIM), bf16),
            pltpu.VMEM((seq // RES16 * STAGE_PITCH, HEAD_DIM), f32),
            pltpu.VMEM((seq, HEAD_DIM), f32),
        ],
        compiler_params=_cparams(("parallel", "parallel"), VMEM_LIMIT_MID),
        name="dilated_attn",
    )(proj3, proj3, proj3, b1, b2, b3)


def _layer_norm(z, g, b):
    mu = jnp.mean(z, axis=-1, keepdims=True)
    zc = z - mu
    var = jnp.mean(zc * zc, axis=-1, keepdims=True)
    return zc * lax.rsqrt(var + LN_EPS) * g + b


def _rms_norm(v, g):
    return v * lax.rsqrt(jnp.mean(v * v, axis=-1, keepdims=True) + LN_EPS) * g


def _pack_bf16_pairs(v):
    n = v.shape[1] // 2
    lo = pltpu.bitcast(v[:, :n].astype(bf16).astype(f32), u32)
    hi = pltpu.bitcast(v[:, n:].astype(bf16).astype(f32), u32)
    return (lo >> 16) | (hi & jnp.uint32(0xFFFF0000))


def _unpack_bf16_pairs(p):
    lo = pltpu.bitcast(p << 16, f32)
    hi = pltpu.bitcast(p & jnp.uint32(0xFFFF0000), f32)
    return jnp.concatenate([lo.astype(bf16), hi.astype(bf16)], axis=1)


LANES = 128
TOK_SUB = D_MODEL // 2 // LANES


def _store_token_tiles(ref, packed, row0=0):
    rows = packed.shape[0]
    for c in range(TOK_SUB):
        ref[pl.ds(TOK_SUB * row0 + c, rows, stride=TOK_SUB), :] = packed[:, c * LANES:(c + 1) * LANES]


def _load_token_tiles_bf16(ref, rows, row0=0):
    los, his = [], []
    for c in range(TOK_SUB):
        p = ref[pl.ds(TOK_SUB * row0 + c, rows, stride=TOK_SUB), :]
        los.append(pltpu.bitcast(p << 16, f32).astype(bf16))
        his.append(pltpu.bitcast(p & jnp.uint32(0xFFFF0000), f32).astype(bf16))
    return jnp.concatenate(los + his, axis=1)


def _outproj_kernel(lru_ref, att_ref, x_ref, gl_ref, ga_ref, w_ref, gm_ref, shf_ref, scf_ref,
                    lng_ref, lnb_ref, wrh_ref, wrl_ref, br_ref,
                    x1_ref, hp_ref, te_ref, tw_ref, rk_ref, cnt_ref, carry_ref):
    tm = x_ref.shape[0]
    step = pl.program_id(0)

    @pl.when(step == 0)
    def _():
        carry_ref[...] = jnp.zeros_like(carry_ref)

    ln = _rms_norm(lru_ref[...].astype(f32), gl_ref[...]).astype(bf16)
    an = _rms_norm(att_ref[...].astype(f32), ga_ref[...]).astype(bf16)
    y = (jnp.dot(ln, w_ref[pl.ds(0, D_LRU), :], preferred_element_type=f32)
         + jnp.dot(an, w_ref[pl.ds(D_LRU, D_ATTN), :], preferred_element_type=f32))
    x1 = _layer_norm(DEEPNORM_ALPHA * x_ref[...] + gm_ref[...] * y, lng_ref[...], lnb_ref[...])
    x1_ref[...] = x1
    hf = x1 * (1.0 + scf_ref[...]) + shf_ref[...]
    _store_token_tiles(hp_ref, _pack_bf16_pairs(hf))

    hf_hi = hf.astype(bf16)
    hf_lo = (hf - hf_hi.astype(f32)).astype(bf16)
    logits = (jnp.dot(hf_hi, wrh_ref[...], preferred_element_type=f32)
              + (jnp.dot(hf_lo, wrh_ref[...], preferred_element_type=f32)
                 + jnp.dot(hf_hi, wrl_ref[...], preferred_element_type=f32))
              + br_ref[...])
    lane = lax.broadcasted_iota(i32, (tm, N_EXPERTS), 1)
    col4 = lax.broadcasted_iota(i32, (tm, TOP_K), 1)
    work = logits
    vals, idxs = [], []
    for _k in range(TOP_K):
        m = jnp.max(work, axis=-1, keepdims=True)
        idx = jnp.min(jnp.where(work == m, lane, N_EXPERTS), axis=-1, keepdims=True)
        vals.append(m)
        idxs.append(idx)
        work = jnp.where(lane == idx, -jnp.inf, work)
    exps = [jnp.exp(v - vals[0]) for v in vals]
    den = exps[0] + exps[1] + exps[2] + exps[3]

    onehot = jnp.zeros((tm, N_EXPERTS), f32)
    for idx in idxs:
        onehot = onehot + (lane == idx).astype(f32)
    ri = lax.broadcasted_iota(i32, (tm, tm), 0)
    ci = lax.broadcasted_iota(i32, (tm, tm), 1)
    lower = (ri > ci).astype(bf16)
    before = jnp.dot(lower, onehot.astype(bf16), preferred_element_type=f32) + carry_ref[...]
    te = jnp.zeros((tm, TOP_K), i32)
    tw = jnp.zeros((tm, TOP_K), f32)
    rk = jnp.zeros((tm, TOP_K), i32)
    for k in range(TOP_K):
        rank_k = jnp.sum(jnp.where(lane == idxs[k], before, 0.0), axis=-1, keepdims=True)
        te = jnp.where(col4 == k, idxs[k], te)
        tw = jnp.where(col4 == k, exps[k] / den, tw)
        rk = jnp.where(col4 == k, rank_k.astype(i32), rk)
    te_ref[...] = te
    tw_ref[...] = tw
    rk_ref[...] = rk
    carry_ref[...] = carry_ref[...] + jnp.sum(onehot, axis=0, keepdims=True)
    cnt_ref[...] = carry_ref[...].astype(i32)


def _out_proj_router(lru2, att2, x2, g_lru, g_attn, w_out_bf, mod3, ln_g, ln_b, w_router, b_router, seq):
    t, d = x2.shape
    tm = ROUTER_TILE
    tps = seq // tm
    row = lambda i: (i, 0)
    const = lambda i: (0, 0)
    wr_hi = w_router.astype(bf16)
    wr_lo = (w_router - wr_hi.astype(f32)).astype(bf16)
    return pl.pallas_call(
        _outproj_kernel,
        out_shape=(
            jax.ShapeDtypeStruct((t, d), f32),
            jax.ShapeDtypeStruct((t * TOK_SUB, LANES), u32),
            jax.ShapeDtypeStruct((t, TOP_K), i32),
            jax.ShapeDtypeStruct((t, TOP_K), f32),
            jax.ShapeDtypeStruct((t, TOP_K), i32),
            jax.ShapeDtypeStruct((1, N_EXPERTS), i32),
        ),
        grid=(t // tm,),
        in_specs=[
            pl.BlockSpec((tm, D_LRU), row),
            pl.BlockSpec((tm, D_ATTN), row),
            pl.BlockSpec((tm, d), row),
            pl.BlockSpec((1, D_LRU), const),
            pl.BlockSpec((1, D_ATTN), const),
            pl.BlockSpec((d, d), const),
            pl.BlockSpec((None, 1, d), lambda i: (i // tps, 0, 2)),
            pl.BlockSpec((None, 1, d), lambda i: (i // tps, 0, 3)),
            pl.BlockSpec((None, 1, d), lambda i: (i // tps, 0, 4)),
            pl.BlockSpec((1, d), const),
            pl.BlockSpec((1, d), const),
            pl.BlockSpec((d, N_EXPERTS), const),
            pl.BlockSpec((d, N_EXPERTS), const),
            pl.BlockSpec((1, N_EXPERTS), const),
        ],
        out_specs=(
            pl.BlockSpec((tm, d), row),
            pl.BlockSpec((tm * TOK_SUB, LANES), row),
            pl.BlockSpec((tm, TOP_K), row),
            pl.BlockSpec((tm, TOP_K), row),
            pl.BlockSpec((tm, TOP_K), row),
            pl.BlockSpec((1, N_EXPERTS), const),
        ),
        scratch_shapes=[pltpu.VMEM((1, N_EXPERTS), f32)],
        compiler_params=_cparams(("arbitrary",), VMEM_LIMIT_BIG),
        name="out_proj_router",
    )(lru2, att2, x2, g_lru.reshape(1, -1), g_attn.reshape(1, -1), w_out_bf, mod3, mod3, mod3,
      ln_g.reshape(1, -1), ln_b.reshape(1, -1), wr_hi, wr_lo, b_router.reshape(1, -1))


def _dispatch_kernel(dest_ref, pad_start_ref, pad_len_ref, nt_ref, hp_ref, xs_ref, zero_ref, sem):
    tm = hp_ref.shape[0] // TOK_SUB
    tile_rows = EXP_TILE * TOK_SUB
    n_tiles = xs_ref.shape[0] // tile_rows

    def token_tile(ref, r):
        return ref.at[pl.ds(pl.multiple_of(r * TOK_SUB, TOK_SUB), TOK_SUB), :]

    def copy(tok, k):
        return pltpu.make_async_copy(token_tile(hp_ref, tok), token_tile(xs_ref, dest_ref[tok * TOP_K + k]), sem)

    def start(tok, c):
        for k in range(TOP_K):
            copy(tok, k).start(priority=k % 2)
        return c

    def wait(tok, c):
        for k in range(TOP_K):
            copy(tok, k).wait()
        return c

    lax.fori_loop(0, tm, start, 0, unroll=2)
    lax.fori_loop(0, tm, wait, 0, unroll=2)

    @pl.when(pl.program_id(0) == pl.num_programs(0) - 1)
    def _():
        zero_ref[...] = jnp.zeros_like(zero_ref)

        def per_expert(e, c):
            p0 = pad_start_ref[e]
            n = pad_len_ref[e]

            def zcopy(r):
                return pltpu.make_async_copy(token_tile(zero_ref, 0), token_tile(xs_ref, p0 + r), sem)

            lax.fori_loop(0, n, lambda r, c2: (zcopy(r).start(), c2)[1], 0)
            lax.fori_loop(0, n, lambda r, c2: (zcopy(r).wait(), c2)[1], 0)
            return c

        lax.fori_loop(0, N_EXPERTS, per_expert, 0)

        def tail_tile(m, c):
            cp = pltpu.make_async_copy(
                zero_ref, xs_ref.at[pl.ds(pl.multiple_of(m * tile_rows, tile_rows), tile_rows), :], sem)
            cp.start()
            cp.wait()
            return c

        lax.fori_loop(nt_ref[0], n_tiles, tail_tile, 0)


def _dispatch(hp, dest_flat, pad_start, pad_len, n_tiles_used, n_rows):
    t = hp.shape[0] // TOK_SUB
    tm = TOK_TILE
    smem = pl.BlockSpec(memory_space=pltpu.SMEM)
    return pl.pallas_call(
        _dispatch_kernel,
        out_shape=jax.ShapeDtypeStruct((n_rows * TOK_SUB, LANES), u32),
        grid=(t // tm,),
        in_specs=[
            pl.BlockSpec((tm * TOP_K,), lambda i: (i,), memory_space=pltpu.SMEM),
            smem, smem, smem,
            pl.BlockSpec((tm * TOK_SUB, LANES), lambda i: (i, 0)),
        ],
        out_specs=pl.BlockSpec(memory_space=pl.ANY),
        scratch_shapes=[pltpu.VMEM((EXP_TILE * TOK_SUB, LANES), u32), pltpu.SemaphoreType.DMA(())],
        compiler_params=_cparams(("arbitrary",), VMEM_LIMIT_MID),
        name="moe_dispatch",
    )(dest_flat, pad_start, pad_len, n_tiles_used, hp)


def _group_start(te_ref, m):
    return jnp.logical_or(m == 0, te_ref[m] != te_ref[jnp.maximum(m - 1, 0)])


def _gate_up_kernel(te_ref, nx_ref, rv_ref, nt_ref, xs_ref, w_hbm, bg_ref, bu_ref, h_ref,
                    wbuf, w_scr, sems, grp_ref, *, layer):
    j = pl.program_id(0)
    m = pl.program_id(1)
    n_pass = pl.num_programs(0)
    fc = w_scr.shape[1] // 2
    half = EXP_TILE // 2
    valid = m < nt_ref[0]

    def copies(e, jj, slot):
        col = pl.multiple_of(jj * fc, fc)
        return (pltpu.make_async_copy(w_hbm.at[layer, e, :, pl.ds(col, fc)], wbuf.at[slot, 0], sems.at[slot, 0]),
                pltpu.make_async_copy(w_hbm.at[layer, e, :, pl.ds(pl.multiple_of(D_FF + col, fc), fc)],
                                      wbuf.at[slot, 1], sems.at[slot, 1]))

    @pl.when(jnp.logical_and(_group_start(te_ref, m), valid))
    def _():
        @pl.when(jnp.logical_and(j == 0, m == 0))
        def _():
            grp_ref[0] = 0
            for c in copies(te_ref[0], 0, 0):
                c.start()

        g = grp_ref[0]
        slot = g & 1
        for c in copies(te_ref[m], j, slot):
            c.wait()
        nxt = nx_ref[m]

        @pl.when(nxt >= 0)
        def _():
            for c in copies(nxt, j, 1 - slot):
                c.start()

        @pl.when(jnp.logical_and(nxt < 0, j + 1 < n_pass))
        def _():
            for c in copies(te_ref[0], j + 1, 1 - slot):
                c.start()

        w_scr[:, pl.ds(0, fc)] = wbuf[slot, 0].astype(bf16)
        w_scr[:, pl.ds(fc, fc)] = wbuf[slot, 1].astype(bf16)
        grp_ref[0] = g + 1

    def swiglu(x):
        gu = jnp.dot(x, w_scr[...], preferred_element_type=f32)
        gate = jnp.minimum(gu[:, :fc] + bg_ref[...], SWIGLU_LIMIT)
        up = jnp.clip(gu[:, fc:] + bu_ref[...], -SWIGLU_LIMIT, SWIGLU_LIMIT)
        glu = gate * jax.nn.sigmoid(gate * SWIGLU_ALPHA)
        return _pack_bf16_pairs((up + 1.0) * glu)

    @pl.when(rv_ref[m] > half)
    def _():
        h_ref[...] = swiglu(_load_token_tiles_bf16(xs_ref, EXP_TILE))

    @pl.when(jnp.logical_and(rv_ref[m] > 0, rv_ref[m] <= half))
    def _():
        h_ref[pl.ds(0, half), :] = swiglu(_load_token_tiles_bf16(xs_ref, half))
        h_ref[pl.ds(half, half), :] = jnp.zeros((half, h_ref.shape[1]), h_ref.dtype)

    @pl.when(rv_ref[m] == 0)
    def _():
        h_ref[...] = jnp.zeros_like(h_ref)


def _gate_up(xs, w_gate_up, b_gate_up4, layer, tile_e, next_e, rows_valid, n_tiles_used):
    p = xs.shape[0] // TOK_SUB
    d = D_MODEL
    fc = D_FF // 2
    n_pass = D_FF // fc
    tm = EXP_TILE
    grid_spec = pltpu.PrefetchScalarGridSpec(
        num_scalar_prefetch=4,
        grid=(n_pass, p // tm),
        in_specs=[
            pl.BlockSpec((tm * TOK_SUB, LANES), lambda j, m, te, nx, rv, nt: (m, 0)),
            pl.BlockSpec(memory_space=pl.ANY),
            pl.BlockSpec((None, None, 1, fc), lambda j, m, te, nx, rv, nt: (layer, te[m], 0, j)),
            pl.BlockSpec((None, None, 1, fc), lambda j, m, te, nx, rv, nt: (layer, te[m], 0, n_pass + j)),
        ],
        out_specs=pl.BlockSpec((tm, fc // 2), lambda j, m, te, nx, rv, nt: (m, j)),
        scratch_shapes=[
            pltpu.VMEM((2, 2, d, fc), f32),
            pltpu.VMEM((d, 2 * fc), bf16),
            pltpu.SemaphoreType.DMA((2, 2)),
            pltpu.SMEM((1,), i32),
        ],
    )
    return pl.pallas_call(
        functools.partial(_gate_up_kernel, layer=layer),
        out_shape=jax.ShapeDtypeStruct((p, D_FF // 2), u32),
        grid_spec=grid_spec,
        compiler_params=_cparams(("arbitrary", "arbitrary"), VMEM_LIMIT_BIG),
        name="moe_gate_up",
    )(tile_e, next_e, rows_valid, n_tiles_used, xs, w_gate_up, b_gate_up4, b_gate_up4)


def _down_kernel(te_ref, nx_ref, rv_ref, nt_ref, h_ref, w_hbm, b_ref, y_ref, wbuf, w_scr, sems, grp_ref, *, layer):
    m = pl.program_id(0)
    fc = D_FF // 2
    half = EXP_TILE // 2
    valid = m < nt_ref[0]

    def copy(e, slot):
        return pltpu.make_async_copy(w_hbm.at[layer, e], wbuf.at[slot], sems.at[slot])

    @pl.when(jnp.logical_and(_group_start(te_ref, m), valid))
    def _():
        @pl.when(m == 0)
        def _():
            grp_ref[0] = 0
            copy(te_ref[0], 0).start()

        g = grp_ref[0]
        slot = g & 1
        copy(te_ref[m], slot).wait()
        nxt = nx_ref[m]

        @pl.when(nxt >= 0)
        def _():
            copy(nxt, 1 - slot).start()

        w_scr[...] = wbuf[slot].astype(bf16)
        grp_ref[0] = g + 1

    def project(hp):
        h = jnp.concatenate([_unpack_bf16_pairs(hp[:, :fc // 2]), _unpack_bf16_pairs(hp[:, fc // 2:])], axis=1)
        return _pack_bf16_pairs(jnp.dot(h, w_scr[...], preferred_element_type=f32) + b_ref[...])

    @pl.when(rv_ref[m] > half)
    def _():
        _store_token_tiles(y_ref, project(h_ref[...]))

    @pl.when(jnp.logical_and(rv_ref[m] > 0, rv_ref[m] <= half))
    def _():
        _store_token_tiles(y_ref, project(h_ref[pl.ds(0, half), :]))
        y_ref[pl.ds(half * TOK_SUB, half * TOK_SUB), :] = jnp.zeros((half * TOK_SUB, LANES), y_ref.dtype)

    @pl.when(rv_ref[m] == 0)
    def _():
        y_ref[...] = jnp.zeros_like(y_ref)


def _down(hs, w_down, b_down4, layer, tile_e, next_e, rows_valid, n_tiles_used):
    p, half = hs.shape
    d_ff = 2 * half
    d = w_down.shape[3]
    tm = EXP_TILE
    grid_spec = pltpu.PrefetchScalarGridSpec(
        num_scalar_prefetch=4,
        grid=(p // tm,),
        in_specs=[
            pl.BlockSpec((tm, half), lambda m, te, nx, rv, nt: (m, 0)),
            pl.BlockSpec(memory_space=pl.ANY),
            pl.BlockSpec((None, None, 1, d), lambda m, te, nx, rv, nt: (layer, te[m], 0, 0)),
        ],
        out_specs=pl.BlockSpec((tm * TOK_SUB, LANES), lambda m, te, nx, rv, nt: (m, 0)),
        scratch_shapes=[
            pltpu.VMEM((2, d_ff, d), f32),
            pltpu.VMEM((d_ff, d), bf16),
            pltpu.SemaphoreType.DMA((2,)),
            pltpu.SMEM((1,), i32),
        ],
    )
    return pl.pallas_call(
        functools.partial(_down_kernel, layer=layer),
        out_shape=jax.ShapeDtypeStruct((p * TOK_SUB, LANES), u32),
        grid_spec=grid_spec,
        compiler_params=_cparams(("arbitrary",), VMEM_LIMIT_BIG),
        name="moe_down",
    )(tile_e, next_e, rows_valid, n_tiles_used, hs, w_down, b_down4)


def _combine_kernel(dest_ref, dest_next_ref, ys_ref, tw_ref, x1_ref, gf_ref, lng_ref, lnb_ref, o_ref,
                    buf_ref, sems):
    tm = x1_ref.shape[0]
    i = pl.program_id(0)
    n = pl.num_programs(0)
    slot = i & 1

    def token_tile(ref, r):
        return ref.at[pl.ds(pl.multiple_of(r * TOK_SUB, TOK_SUB), TOK_SUB), :]

    def copy(dref, s, tok, k):
        return pltpu.make_async_copy(token_tile(ys_ref, dref[tok * TOP_K + k]),
                                     token_tile(buf_ref, (s * TOP_K + k) * tm + tok), sems.at[s])

    def start_all(dref, s):
        def body(tok, c):
            for k in range(TOP_K):
                copy(dref, s, tok, k).start(priority=k % 2)
            return c
        lax.fori_loop(0, tm, body, 0, unroll=2)

    @pl.when(i == 0)
    def _():
        start_all(dest_ref, 0)

    @pl.when(i + 1 < n)
    def _():
        start_all(dest_next_ref, 1 - slot)

    def wait_body(tok, c):
        for k in range(TOP_K):
            copy(dest_ref, slot, tok, k).wait()
        return c

    lax.fori_loop(0, tm, wait_body, 0, unroll=2)

    tw = tw_ref[...]
    y = jnp.zeros((tm, D_MODEL), f32)
    for k in range(TOP_K):
        y = y + tw[:, k:k + 1] * _load_token_tiles_bf16(buf_ref, tm, row0=(slot * TOP_K + k) * tm).astype(f32)
    o_ref[...] = _layer_norm(DEEPNORM_ALPHA * x1_ref[...] + gf_ref[...] * y, lng_ref[...], lnb_ref[...])


def _combine(ys, dest_flat, top_w, x1, mod3, ln_g, ln_b, seq):
    t, d = x1.shape
    tm = TOK_TILE
    tps = seq // tm
    n_steps = t // tm
    grid_spec = pltpu.PrefetchScalarGridSpec(
        num_scalar_prefetch=0,
        grid=(n_steps,),
        in_specs=[
            pl.BlockSpec((tm * TOP_K,), lambda i: (i,), memory_space=pltpu.SMEM),
            pl.BlockSpec((tm * TOP_K,), lambda i: (jnp.minimum(i + 1, n_steps - 1),), memory_space=pltpu.SMEM),
            pl.BlockSpec(memory_space=pl.ANY),
            pl.BlockSpec((tm, TOP_K), lambda i: (i, 0)),
            pl.BlockSpec((tm, d), lambda i: (i, 0)),
            pl.BlockSpec((None, 1, d), lambda i: (i // tps, 0, 5)),
            pl.BlockSpec((1, d), lambda i: (0, 0)),
            pl.BlockSpec((1, d), lambda i: (0, 0)),
        ],
        out_specs=pl.BlockSpec((tm, d), lambda i: (i, 0)),
        scratch_shapes=[pltpu.VMEM((2 * TOP_K * tm * TOK_SUB, LANES), u32), pltpu.SemaphoreType.DMA((2,))],
    )
    return pl.pallas_call(
        _combine_kernel,
        out_shape=jax.ShapeDtypeStruct((t, d), f32),
        grid_spec=grid_spec,
        compiler_params=_cparams(("arbitrary",), VMEM_LIMIT_MID),
        name="moe_combine",
    )(dest_flat, dest_flat, ys, top_w, x1, mod3, ln_g.reshape(1, -1), ln_b.reshape(1, -1))


def _routing_tables(top_e, rank, counts, n_tiles):
    tm = EXP_TILE
    counts = counts.reshape(-1)
    padded = (counts + tm - 1) // tm * tm
    pend = jnp.cumsum(padded)
    pstart = pend - padded
    dest = (pstart[top_e] + rank).reshape(-1).astype(i32)
    tile_start = jnp.arange(n_tiles, dtype=i32) * tm
    tile_e = jnp.sum((pend[None, :] <= tile_start[:, None]).astype(i32), axis=1)
    tile_e = jnp.minimum(tile_e, N_EXPERTS - 1).astype(i32)
    n_used = (pend[-1] // tm).astype(i32).reshape(1)
    mine = jnp.arange(N_EXPERTS, dtype=i32)[None, :] == tile_e[:, None]
    group_end = jnp.sum(jnp.where(mine, pend[None, :], 0), axis=1)
    data_end = jnp.sum(jnp.where(mine, (pstart + counts)[None, :], 0), axis=1)
    rows_valid = jnp.where(tile_start < pend[-1], jnp.clip(data_end - tile_start, 0, tm), 0).astype(i32)
    next_e = jnp.sum((pend[None, :] <= group_end[:, None]).astype(i32), axis=1)
    next_e = jnp.where(group_end < pend[-1], jnp.minimum(next_e, N_EXPERTS - 1), -1).astype(i32)
    return dest, tile_e, next_e, rows_valid, n_used, (pstart + counts).astype(i32), (padded - counts).astype(i32)


def kernel(x, c, rel_bias, w_ada, b_ada, w_in, conv_w, conv_b, lru_w_a, lru_b_a, lru_w_x, lru_b_x, lru_lambda, norm_lru_g, norm_attn_g, w_out, ln_mix_g, ln_mix_b, w_router, b_router, w_gate_up, b_gate_up, w_down, b_down, ln_ffn_g, ln_ffn_b):
    nb, seq, d = x.shape
    t = nb * seq
    mod = _ada_mod(c, w_ada, b_ada)
    b1, b2, b3 = _attn_bias_tables(rel_bias)
    n_tiles = t * TOP_K // EXP_TILE + N_EXPERTS
    q_lo = 2 * D_LRU
    x2 = x.reshape(t, d)
    for l in range(DEPTH):
        mod3 = mod[l].reshape(nb, 1, 6 * d)
        w_in_l = w_in[l].at[:, q_lo:q_lo + D_ATTN].multiply(HEAD_DIM ** -0.5).astype(bf16)
        proj = _in_proj(x2, mod3, w_in_l, seq).reshape(nb, seq, D_IN)
        wg = jnp.concatenate([lru_w_a[l], lru_w_x[l]], axis=-1).astype(bf16)
        bg = jnp.concatenate([lru_b_a[l].reshape(2, N_LRU_BLOCKS, 1, LRU_BLOCK),
                              lru_b_x[l].reshape(2, N_LRU_BLOCKS, 1, LRU_BLOCK)], axis=-1)
        lru = _lru_group(proj, conv_w[l], conv_b[l], wg, bg, lru_lambda[l])
        att = _dilated_attention(proj, b1, b2, b3)
        x1, hp, top_e, top_w, rank, counts = _out_proj_router(
            lru.reshape(t, D_LRU), att.reshape(t, D_ATTN), x2, norm_lru_g[l], norm_attn_g[l],
            w_out[l].astype(bf16), mod3, ln_mix_g[l], ln_mix_b[l], w_router[l], b_router[l], seq)
        dest, tile_e, next_e, rows_valid, n_used, pad_start, pad_len = _routing_tables(top_e, rank, counts, n_tiles)
        xs = _dispatch(hp, dest, pad_start, pad_len, n_used, n_tiles * EXP_TILE)
        hs = _gate_up(xs, w_gate_up, b_gate_up.reshape(DEPTH, N_EXPERTS, 1, 2 * D_FF), l,
                      tile_e, next_e, rows_valid, n_used)
        ys = _down(hs, w_down, b_down.reshape(DEPTH, N_EXPERTS, 1, d), l, tile_e, next_e, rows_valid, n_used)
        x2 = _combine(ys, dest, top_w, x1, mod3, ln_ffn_g[l], ln_ffn_b[l], seq)
    return x2.reshape(nb, seq, d)
```

```python
import functools
import math

import jax
import jax.numpy as jnp
from jax import lax
from jax.experimental import pallas as pl
from jax.experimental.pallas import tpu as pltpu

f32 = jnp.float32
bf16 = jnp.bfloat16
i32 = jnp.int32
u32 = jnp.uint32

D_MODEL = 2048
DEPTH = 2
D_LRU = 1024
N_LRU_BLOCKS = 8
LRU_BLOCK = 128
CONV_WIDTH = 4
LRU_C = 8.0
D_ATTN = 1024
HEAD_DIM = 128
N_HEADS = 8
D_IN = 2 * D_LRU + 3 * D_ATTN
NEG_INF = -1e30
N_BUCKETS = 32
MAX_EXACT = 8
MAX_DISTANCE = 1024
N_EXPERTS = 32
TOP_K = 4
D_FF = D_MODEL
SWIGLU_LIMIT = 7.0
SWIGLU_ALPHA = 1.702
DEEPNORM_ALPHA = (2 * DEPTH) ** 0.25
LN_EPS = 1e-5

VMEM_LIMIT_BIG = 56 * 1024 * 1024
VMEM_LIMIT_MID = 40 * 1024 * 1024

ROW_TILE = 512
ROUTER_TILE = 512
TOK_TILE = 256
EXP_TILE = 512
LRU_CHUNK = 128
Q_BLOCK = 128
NAT_BLOCK = 256
SUB_ROWS = 128
HALF = 64
RES16 = 16
STAGE_PITCH = 24


def _cparams(sem, vmem=None):
    return pltpu.CompilerParams(dimension_semantics=sem, vmem_limit_bytes=vmem)


def _ada_kernel(c_ref, w_ref, b_ref, o_ref):
    c = c_ref[...]
    ca = c * jax.nn.sigmoid(c)
    o_ref[0] = jnp.dot(ca, w_ref[0], precision=lax.Precision.HIGHEST,
                       preferred_element_type=f32) + b_ref[0]


def _ada_mod(c, w_ada, b_ada):
    depth, d, n = w_ada.shape
    b = c.shape[0]
    tn = 1024
    return pl.pallas_call(
        _ada_kernel,
        out_shape=jax.ShapeDtypeStruct((depth, b, n), f32),
        grid=(depth, n // tn),
        in_specs=[
            pl.BlockSpec((b, d), lambda l, j: (0, 0)),
            pl.BlockSpec((1, d, tn), lambda l, j: (l, 0, j)),
            pl.BlockSpec((1, 1, tn), lambda l, j: (l, 0, j)),
        ],
        out_specs=pl.BlockSpec((1, b, tn), lambda l, j: (l, 0, j)),
        compiler_params=_cparams(("parallel", "parallel"), VMEM_LIMIT_MID),
        name="ada_mod",
    )(c, w_ada, b_ada.reshape(depth, 1, n))


def _inproj_kernel(x_ref, sh_ref, sc_ref, w_ref, o_ref):
    h = x_ref[...] * (1.0 + sc_ref[...]) + sh_ref[...]
    o_ref[...] = jnp.dot(h.astype(bf16), w_ref[...], preferred_element_type=f32).astype(o_ref.dtype)


def _in_proj(x2, mod3, w_in_bf, seq):
    t, d = x2.shape
    n = w_in_bf.shape[1]
    tn = n // 2
    tiles_per_seq = seq // ROW_TILE
    return pl.pallas_call(
        _inproj_kernel,
        out_shape=jax.ShapeDtypeStruct((t, n), bf16),
        grid=(n // tn, t // ROW_TILE),
        in_specs=[
            pl.BlockSpec((ROW_TILE, d), lambda j, i: (i, 0)),
            pl.BlockSpec((None, 1, d), lambda j, i: (i // tiles_per_seq, 0, 0)),
            pl.BlockSpec((None, 1, d), lambda j, i: (i // tiles_per_seq, 0, 1)),
            pl.BlockSpec((d, tn), lambda j, i: (0, j)),
        ],
        out_specs=pl.BlockSpec((ROW_TILE, tn), lambda j, i: (i, j)),
        compiler_params=_cparams(("parallel", "parallel"), VMEM_LIMIT_BIG),
        name="in_proj",
    )(x2, mod3, mod3, w_in_bf)


def _lru_kernel(xb_ref, gb_ref, cw_ref, cb_ref, wg_ref, bg_ref, lam_ref, o_ref,
                xt_ref, hs_ref, a_ref, b_ref, hb_ref):
    nb, seq, _ = xb_ref.shape
    tc = LRU_CHUNK
    rows = tc * nb
    n_chunks = seq // tc
    front = (CONV_WIDTH // 2) * nb

    xt_ref[pl.ds(0, front), :] = jnp.zeros((front, LRU_BLOCK), f32)
    xt_ref[pl.ds(front + seq * nb, nb), :] = jnp.zeros((nb, LRU_BLOCK), f32)
    for b in range(nb):
        xt_ref[pl.ds(front + b, seq, stride=nb), :] = xb_ref[b].astype(f32)

    lam = lam_ref[...]
    neg_c_sp = -LRU_C * jax.nn.softplus(-lam)
    cw = cw_ref[...]
    cb = cb_ref[...]

    def gates(d, r0):
        xc = cb
        for j in range(CONV_WIDTH):
            xc = xc + cw[j:j + 1, :] * xt_ref[pl.ds(pl.multiple_of(r0 + j * nb, nb), rows), :]
        g = jnp.dot(xc.astype(bf16), wg_ref[d], preferred_element_type=f32) + bg_ref[d]
        r = 0.5 + 0.5 * jnp.tanh(0.5 * g[:, :LRU_BLOCK])
        i = 0.5 + 0.5 * jnp.tanh(0.5 * g[:, LRU_BLOCK:])
        log_a = neg_c_sp[d:d + 1, :] * r
        a = jnp.exp(log_a)
        u = a * a
        z = 2.0 * log_a
        one_minus = jnp.where(u == 1.0, -z, (1.0 - u) * z / jnp.log(u))
        root = jnp.where(one_minus > 0.0, one_minus * lax.rsqrt(one_minus), 0.0)
        a_ref[...] = a
        b_ref[...] = root * (i * xc)

    def fwd_chunk(c, h):
        r0 = pl.multiple_of(c * rows, rows)
        gates(0, r0)

        def step(t, h):
            rr = pl.multiple_of(t * nb, nb)
            h = a_ref[pl.ds(rr, nb), :] * h + b_ref[pl.ds(rr, nb), :]
            hs_ref[pl.ds(r0 + rr, nb), :] = h
            return h

        return lax.fori_loop(0, tc, step, h, unroll=8)

    lax.fori_loop(0, n_chunks, fwd_chunk, jnp.zeros((nb, LRU_BLOCK), f32))

    def bwd_chunk(ci, h):
        c = n_chunks - 1 - ci
        r0 = pl.multiple_of(c * rows, rows)
        gates(1, r0)

        def step(ti, h):
            rr = pl.multiple_of((tc - 1 - ti) * nb, nb)
            h = a_ref[pl.ds(rr, nb), :] * h + b_ref[pl.ds(rr, nb), :]
            hb_ref[pl.ds(rr, nb), :] = h
            return h

        h = lax.fori_loop(0, tc, step, h, unroll=8)
        hs_ref[pl.ds(r0, rows), :] = hs_ref[pl.ds(r0, rows), :] + hb_ref[...]
        t0 = pl.multiple_of(c * tc, tc)
        for b in range(nb):
            hr = hs_ref[pl.ds(r0 + b, tc, stride=nb), :]
            gate = jax.nn.gelu(gb_ref[b, pl.ds(t0, tc), :].astype(f32))
            o_ref[b, pl.ds(t0, tc), :] = (hr * gate).astype(o_ref.dtype)
        return h

    lax.fori_loop(0, n_chunks, bwd_chunk, jnp.zeros((nb, LRU_BLOCK), f32))


def _lru_group(proj3, conv_w, conv_b, wg, bg, lam):
    nb, seq, _ = proj3.shape
    g = N_LRU_BLOCKS
    return pl.pallas_call(
        _lru_kernel,
        out_shape=jax.ShapeDtypeStruct((nb, seq, D_LRU), bf16),
        grid=(g,),
        in_specs=[
            pl.BlockSpec((nb, seq, LRU_BLOCK), lambda j: (0, 0, j)),
            pl.BlockSpec((nb, seq, LRU_BLOCK), lambda j: (0, 0, g + j)),
            pl.BlockSpec((CONV_WIDTH, LRU_BLOCK), lambda j: (0, j)),
            pl.BlockSpec((1, LRU_BLOCK), lambda j: (0, j)),
            pl.BlockSpec((2, None, LRU_BLOCK, 2 * LRU_BLOCK), lambda j: (0, j, 0, 0)),
            pl.BlockSpec((2, None, 1, 2 * LRU_BLOCK), lambda j: (0, j, 0, 0)),
            pl.BlockSpec((2, LRU_BLOCK), lambda j: (0, j)),
        ],
        out_specs=pl.BlockSpec((nb, seq, LRU_BLOCK), lambda j: (0, 0, j)),
        scratch_shapes=[
            pltpu.VMEM(((seq + CONV_WIDTH - 1) * nb, LRU_BLOCK), f32),
            pltpu.VMEM((seq * nb, LRU_BLOCK), f32),
            pltpu.VMEM((LRU_CHUNK * nb, LRU_BLOCK), f32),
            pltpu.VMEM((LRU_CHUNK * nb, LRU_BLOCK), f32),
            pltpu.VMEM((LRU_CHUNK * nb, LRU_BLOCK), f32),
        ],
        compiler_params=_cparams(("parallel",), VMEM_LIMIT_BIG),
        name="rg_lru",
    )(proj3, proj3, conv_w, conv_b.reshape(1, D_LRU), wg, bg, lam)


def _t5_bucket(rel):
    nbk = N_BUCKETS // 2
    ret = jnp.where(rel > 0, nbk, 0)
    n = jnp.abs(rel)
    nf = jnp.maximum(n, 1).astype(f32)
    large = MAX_EXACT + (jnp.log(nf / MAX_EXACT) / math.log(MAX_DISTANCE / MAX_EXACT)
                         * (nbk - MAX_EXACT)).astype(i32)
    large = jnp.minimum(large, nbk - 1)
    return ret + jnp.where(n < MAX_EXACT, n, large)


def _attn_bias_tables(rel_bias):
    qb = Q_BLOCK
    rb = rel_bias.astype(f32)
    i = jnp.arange(qb, dtype=i32)[:, None]

    def tile(step, dilation, valid):
        onehot = jax.nn.one_hot(_t5_bucket(step * dilation), N_BUCKETS, dtype=f32)
        t = jnp.einsum("qkn,nh->hqk", onehot, rb, precision=lax.Precision.HIGHEST)
        return jnp.where(valid[None], t, NEG_INF)

    i1 = jnp.arange(NAT_BLOCK, dtype=i32)[:, None]
    win = NAT_BLOCK + 2 * HALF
    j = jnp.arange(win, dtype=i32)[None, :]
    step1 = j - HALF - i1
    band1 = jnp.abs(step1) <= HALF
    b1 = jnp.stack([tile(step1, 1, band1),
                    tile(step1, 1, band1 & (j >= HALF)),
                    tile(step1, 1, band1 & (j < win - HALF))],
                   axis=1)
    lk = jnp.arange(qb, dtype=i32)[None, :]
    step2 = jnp.concatenate(
        [jnp.concatenate([4 * (lk - i) + (jk - jq) for jk in range(4)], axis=1) for jq in range(4)], axis=0)
    b2 = tile(step2, 4, jnp.abs(step2) <= HALF)
    step3 = lk - i
    b3 = tile(step3, 16, jnp.abs(step3) <= HALF)
    return b1, b2, b3


def _lane_tiles(a):
    return [a[:, c:c + LANES] for c in range(0, a.shape[1], LANES)] if a.shape[1] % LANES == 0 else [a]


def _softmax_parts(s):
    m = jnp.max(functools.reduce(jnp.maximum, _lane_tiles(s)), axis=-1, keepdims=True)
    p = jnp.exp(s - m)
    return m, p, jnp.sum(functools.reduce(jnp.add, _lane_tiles(p)), axis=-1, keepdims=True)


def _attn_kernel(q_ref, k_ref, v_ref, b1_ref, b2_ref, b3_ref, o_ref,
                 stage_ref, qp_ref, kp_ref, vp_ref, kpad_ref, vpad_ref, o23_ref, l23_ref):
    seq = q_ref.shape[0]
    qb = NAT_BLOCK
    nblk = seq // qb
    per_res = seq // RES16
    cls = 4 * per_res
    dn = (((1,), (1,)), ((), ()))

    def perm_row0(r16):
        return ((r16 % 4) * 4 + r16 // 4) * per_res

    def pitched(ref, g, rows=RES16):
        return ref.at[pl.ds(pl.multiple_of(g * STAGE_PITCH, 8), rows), :]

    for src, dst in ((q_ref, qp_ref), (k_ref, kp_ref), (v_ref, vp_ref)):
        def fill(g, c, src=src):
            pitched(stage_ref, g)[...] = src[pl.ds(pl.multiple_of(g * RES16, RES16), RES16), :].astype(f32)
            return c

        lax.fori_loop(0, per_res, fill, 0, unroll=8)
        for r in range(RES16):
            dst[pl.ds(perm_row0(r), per_res), :] = stage_ref[pl.ds(r, per_res, stride=STAGE_PITCH), :].astype(bf16)

    zpad = jnp.zeros((HALF, HEAD_DIM), bf16)
    kpad_ref[pl.ds(0, HALF), :] = zpad
    vpad_ref[pl.ds(0, HALF), :] = zpad
    kpad_ref[pl.ds(HALF + seq, HALF), :] = zpad
    vpad_ref[pl.ds(HALF + seq, HALF), :] = zpad
    kpad_ref[pl.ds(HALF, seq), :] = k_ref[...]
    vpad_ref[pl.ds(HALF, seq), :] = v_ref[...]

    for r4 in range(4):
        base = r4 * cls
        q4 = qp_ref[pl.ds(base, cls), :]
        k4 = kp_ref[pl.ds(base, cls), :]
        v4 = vp_ref[pl.ds(base, cls), :]
        raw = lax.dot_general(q4, k4, dn, preferred_element_type=f32)
        m2, p2, d2 = _softmax_parts(raw + b2_ref[...])
        o2 = jnp.dot(p2.astype(bf16), v4, preferred_element_type=f32)
        m3s, d3s, o3s = [], [], []
        for jq in range(4):
            lo, hi = jq * per_res, (jq + 1) * per_res
            m3, p3, d3 = _softmax_parts(raw[lo:hi, lo:hi] + b3_ref[...])
            m3s.append(m3)
            d3s.append(d3)
            o3s.append(jnp.dot(p3.astype(bf16), v4[lo:hi], preferred_element_type=f32))
        m3 = jnp.concatenate(m3s, axis=0)
        d3 = jnp.concatenate(d3s, axis=0)
        o3 = jnp.concatenate(o3s, axis=0)
        m23 = jnp.maximum(m2, m3)
        w2 = jnp.exp(m2 - m23)
        w3 = jnp.exp(m3 - m23)
        d23 = w2 * d2 + w3 * d3
        o23_ref[pl.ds(base, cls), :] = (w2 * o2 + w3 * o3) / d23
        l23_ref[pl.ds(base, cls), :] = jnp.broadcast_to(m23 + jnp.log(d23), (cls, HEAD_DIM))

    for r in range(RES16):
        stage_ref[pl.ds(r, per_res, stride=STAGE_PITCH), :] = o23_ref[pl.ds(perm_row0(r), per_res), :]
    for r in range(RES16):
        o23_ref[pl.ds(r, per_res, stride=STAGE_PITCH), :] = l23_ref[pl.ds(perm_row0(r), per_res), :]

    def natural_rows(ref, g0, rows):
        return jnp.concatenate([pitched(ref, g0 + i)[...] for i in range(rows // RES16)], axis=0)

    per_group = 2
    n_groups = nblk // per_group

    def nat_group(gi, carry):
        for sb in range(per_group):
            row0 = pl.multiple_of((gi * per_group + sb) * qb, qb)
            q = q_ref[pl.ds(row0, qb), :]
            kw = kpad_ref[pl.ds(row0, qb + 2 * HALF), :]
            vw = vpad_ref[pl.ds(row0, qb + 2 * HALF), :]
            if sb == 0:
                bias = b1_ref[jnp.where(gi == 0, 1, 0)]
            elif sb == per_group - 1:
                bias = b1_ref[jnp.where(gi == n_groups - 1, 2, 0)]
            else:
                bias = b1_ref[0]
            s1 = lax.dot_general(q, kw, dn, preferred_element_type=f32) + bias
            m1, p1, d1 = _softmax_parts(s1)
            o1 = jnp.dot(p1.astype(bf16), vw, preferred_element_type=f32)
            g0 = (gi * per_group + sb) * (qb // RES16)
            l23 = natural_rows(o23_ref, g0, qb)
            on23 = natural_rows(stage_ref, g0, qb)
            mm = jnp.maximum(m1, l23)
            w1 = jnp.exp(m1 - mm)
            w23 = jnp.exp(l23 - mm)
            o_ref[pl.ds(row0, qb), :] = ((w1 * o1 + w23 * on23) / (w1 * d1 + w23)).astype(o_ref.dtype)
        return carry

    lax.fori_loop(0, n_groups, nat_group, 0)


def _dilated_attention(proj3, b1, b2, b3):
    nb, seq, _ = proj3.shape
    assert seq // RES16 == Q_BLOCK and seq % (2 * NAT_BLOCK) == 0 and Q_BLOCK % SUB_ROWS == 0
    q0 = 2 * D_LRU // HEAD_DIM
    k0 = q0 + N_HEADS
    v0 = k0 + N_HEADS
    return pl.pallas_call(
        _attn_kernel,
        out_shape=jax.ShapeDtypeStruct((nb, seq, D_ATTN), bf16),
        grid=(N_HEADS, nb),
        in_specs=[
            pl.BlockSpec((None, seq, HEAD_DIM), lambda h, b: (b, 0, q0 + h)),
            pl.BlockSpec((None, seq, HEAD_DIM), lambda h, b: (b, 0, k0 + h)),
            pl.BlockSpec((None, seq, HEAD_DIM), lambda h, b: (b, 0, v0 + h)),
            pl.BlockSpec((None, 3, NAT_BLOCK, NAT_BLOCK + 2 * HALF), lambda h, b: (h, 0, 0, 0)),
            pl.BlockSpec((None, 4 * Q_BLOCK, 4 * Q_BLOCK), lambda h, b: (h, 0, 0)),
            pl.BlockSpec((None, Q_BLOCK, Q_BLOCK), lambda h, b: (h, 0, 0)),
        ],
        out_specs=pl.BlockSpec((None, seq, HEAD_DIM), lambda h, b: (b, 0, h)),
        scratch_shapes=[
            pltpu.VMEM((seq // RES16 * STAGE_PITCH, HEAD_DIM), f32),
            pltpu.VMEM((seq, HEAD_DIM), bf16),
            pltpu.VMEM((seq, HEAD_DIM), bf16),
            pltpu.VMEM((seq, HEAD_DIM), bf16),
            pltpu.VMEM((seq + 2 * HALF, HEAD_DIM), bf16),
            pltpu.VMEM((seq + 2 * HALF, HEAD_DIM), bf16),
            pltpu.VMEM((seq // RES16 * STAGE_PITCH, HEAD_DIM), f32),
            pltpu.VMEM((seq, HEAD_DIM), f32),
        ],
        compiler_params=_cparams(("parallel", "parallel"), VMEM_LIMIT_MID),
        name="dilated_attn",
    )(proj3, proj3, proj3, b1, b2, b3)


def _layer_norm(z, g, b):
    mu = jnp.mean(z, axis=-1, keepdims=True)
    zc = z - mu
    var = jnp.mean(zc * zc, axis=-1, keepdims=True)
    return zc * lax.rsqrt(var + LN_EPS) * g + b


def _rms_norm(v, g):
    return v * lax.rsqrt(jnp.mean(v * v, axis=-1, keepdims=True) + LN_EPS) * g


def _pack_bf16_pairs(v):
    n = v.shape[1] // 2
    lo = pltpu.bitcast(v[:, :n].astype(bf16).astype(f32), u32)
    hi = pltpu.bitcast(v[:, n:].astype(bf16).astype(f32), u32)
    return (lo >> 16) | (hi & jnp.uint32(0xFFFF0000))


def _unpack_bf16_pairs(p):
    lo = pltpu.bitcast(p << 16, f32)
    hi = pltpu.bitcast(p & jnp.uint32(0xFFFF0000), f32)
    return jnp.concatenate([lo.astype(bf16), hi.astype(bf16)], axis=1)


LANES = 128
TOK_SUB = D_MODEL // 2 // LANES


def _store_token_tiles(ref, packed, row0=0):
    rows = packed.shape[0]
    for c in range(TOK_SUB):
        ref[pl.ds(TOK_SUB * row0 + c, rows, stride=TOK_SUB), :] = packed[:, c * LANES:(c + 1) * LANES]


def _load_token_tiles_bf16(ref, rows, row0=0):
    los, his = [], []
    for c in range(TOK_SUB):
        p = ref[pl.ds(TOK_SUB * row0 + c, rows, stride=TOK_SUB), :]
        los.append(pltpu.bitcast(p << 16, f32).astype(bf16))
        his.append(pltpu.bitcast(p & jnp.uint32(0xFFFF0000), f32).astype(bf16))
    return jnp.concatenate(los + his, axis=1)


def _outproj_kernel(lru_ref, att_ref, x_ref, gl_ref, ga_ref, w_ref, gm_ref, shf_ref, scf_ref,
                    lng_ref, lnb_ref, wrh_ref, wrl_ref, br_ref,
                    x1_ref, hp_ref, te_ref, tw_ref, rk_ref, cnt_ref, carry_ref):
    tm = x_ref.shape[0]
    step = pl.program_id(0)

    @pl.when(step == 0)
    def _():
        carry_ref[...] = jnp.zeros_like(carry_ref)

    ln = _rms_norm(lru_ref[...].astype(f32), gl_ref[...]).astype(bf16)
    an = _rms_norm(att_ref[...].astype(f32), ga_ref[...]).astype(bf16)
    y = (jnp.dot(ln, w_ref[pl.ds(0, D_LRU), :], preferred_element_type=f32)
         + jnp.dot(an, w_ref[pl.ds(D_LRU, D_ATTN), :], preferred_element_type=f32))
    x1 = _layer_norm(DEEPNORM_ALPHA * x_ref[...] + gm_ref[...] * y, lng_ref[...], lnb_ref[...])
    x1_ref[...] = x1
    hf = x1 * (1.0 + scf_ref[...]) + shf_ref[...]
    _store_token_tiles(hp_ref, _pack_bf16_pairs(hf))

    hf_hi = hf.astype(bf16)
    hf_lo = (hf - hf_hi.astype(f32)).astype(bf16)
    logits = (jnp.dot(hf_hi, wrh_ref[...], preferred_element_type=f32)
              + (jnp.dot(hf_lo, wrh_ref[...], preferred_element_type=f32)
                 + jnp.dot(hf_hi, wrl_ref[...], preferred_element_type=f32))
              + br_ref[...])
    lane = lax.broadcasted_iota(i32, (tm, N_EXPERTS), 1)
    col4 = lax.broadcasted_iota(i32, (tm, TOP_K), 1)
    work = logits
    vals, idxs = [], []
    for _k in range(TOP_K):
        m = jnp.max(work, axis=-1, keepdims=True)
        idx = jnp.min(jnp.where(work == m, lane, N_EXPERTS), axis=-1, keepdims=True)
        vals.append(m)
        idxs.append(idx)
        work = jnp.where(lane == idx, -jnp.inf, work)
    exps = [jnp.exp(v - vals[0]) for v in vals]
    den = exps[0] + exps[1] + exps[2] + exps[3]

    onehot = jnp.zeros((tm, N_EXPERTS), f32)
    for idx in idxs:
        onehot = onehot + (lane == idx).astype(f32)
    ri = lax.broadcasted_iota(i32, (tm, tm), 0)
    ci = lax.broadcasted_iota(i32, (tm, tm), 1)
    lower = (ri > ci).astype(bf16)
    before = jnp.dot(lower, onehot.astype(bf16), preferred_element_type=f32) + carry_ref[...]
    te = jnp.zeros((tm, TOP_K), i32)
    tw = jnp.zeros((tm, TOP_K), f32)
    rk = jnp.zeros((tm, TOP_K), i32)
    for k in range(TOP_K):
        rank_k = jnp.sum(jnp.where(lane == idxs[k], before, 0.0), axis=-1, keepdims=True)
        te = jnp.where(col4 == k, idxs[k], te)
        tw = jnp.where(col4 == k, exps[k] / den, tw)
        rk = jnp.where(col4 == k, rank_k.astype(i32), rk)
    te_ref[...] = te
    tw_ref[...] = tw
    rk_ref[...] = rk
    carry_ref[...] = carry_ref[...] + jnp.sum(onehot, axis=0, keepdims=True)
    cnt_ref[...] = carry_ref[...].astype(i32)


def _out_proj_router(lru2, att2, x2, g_lru, g_attn, w_out_bf, mod3, ln_g, ln_b, w_router, b_router, seq):
    t, d = x2.shape
    tm = ROUTER_TILE
    tps = seq // tm
    row = lambda i: (i, 0)
    const = lambda i: (0, 0)
    wr_hi = w_router.astype(bf16)
    wr_lo = (w_router - wr_hi.astype(f32)).astype(bf16)
    return pl.pallas_call(
        _outproj_kernel,
        out_shape=(
            jax.ShapeDtypeStruct((t, d), f32),
            jax.ShapeDtypeStruct((t * TOK_SUB, LANES), u32),
            jax.ShapeDtypeStruct((t, TOP_K), i32),
            jax.ShapeDtypeStruct((t, TOP_K), f32),
            jax.ShapeDtypeStruct((t, TOP_K), i32),
            jax.ShapeDtypeStruct((1, N_EXPERTS), i32),
        ),
        grid=(t // tm,),
        in_specs=[
            pl.BlockSpec((tm, D_LRU), row),
            pl.BlockSpec((tm, D_ATTN), row),
            pl.BlockSpec((tm, d), row),
            pl.BlockSpec((1, D_LRU), const),
            pl.BlockSpec((1, D_ATTN), const),
            pl.BlockSpec((d, d), const),
            pl.BlockSpec((None, 1, d), lambda i: (i // tps, 0, 2)),
            pl.BlockSpec((None, 1, d), lambda i: (i // tps, 0, 3)),
            pl.BlockSpec((None, 1, d), lambda i: (i // tps, 0, 4)),
            pl.BlockSpec((1, d), const),
            pl.BlockSpec((1, d), const),
            pl.BlockSpec((d, N_EXPERTS), const),
            pl.BlockSpec((d, N_EXPERTS), const),
            pl.BlockSpec((1, N_EXPERTS), const),
        ],
        out_specs=(
            pl.BlockSpec((tm, d), row),
            pl.BlockSpec((tm * TOK_SUB, LANES), row),
            pl.BlockSpec((tm, TOP_K), row),
            pl.BlockSpec((tm, TOP_K), row),
            pl.BlockSpec((tm, TOP_K), row),
            pl.BlockSpec((1, N_EXPERTS), const),
        ),
        scratch_shapes=[pltpu.VMEM((1, N_EXPERTS), f32)],
        compiler_params=_cparams(("arbitrary",), VMEM_LIMIT_BIG),
        name="out_proj_router",
    )(lru2, att2, x2, g_lru.reshape(1, -1), g_attn.reshape(1, -1), w_out_bf, mod3, mod3, mod3,
      ln_g.reshape(1, -1), ln_b.reshape(1, -1), wr_hi, wr_lo, b_router.reshape(1, -1))


def _dispatch_kernel(dest_ref, pad_start_ref, pad_len_ref, nt_ref, hp_ref, xs_ref, zero_ref, sem):
    tm = hp_ref.shape[0] // TOK_SUB
    tile_rows = EXP_TILE * TOK_SUB
    n_tiles = xs_ref.shape[0] // tile_rows

    def token_tile(ref, r):
        return ref.at[pl.ds(pl.multiple_of(r * TOK_SUB, TOK_SUB), TOK_SUB), :]

    def copy(tok, k):
        return pltpu.make_async_copy(token_tile(hp_ref, tok), token_tile(xs_ref, dest_ref[tok * TOP_K + k]), sem)

    def start(tok, c):
        for k in range(TOP_K):
            copy(tok, k).start(priority=k % 2)
        return c

    def wait(tok, c):
        for k in range(TOP_K):
            copy(tok, k).wait()
        return c

    lax.fori_loop(0, tm, start, 0, unroll=2)
    lax.fori_loop(0, tm, wait, 0, unroll=2)

    @pl.when(pl.program_id(0) == pl.num_programs(0) - 1)
    def _():
        zero_ref[...] = jnp.zeros_like(zero_ref)

        def per_expert(e, c):
            p0 = pad_start_ref[e]
            n = pad_len_ref[e]

            def zcopy(r):
                return pltpu.make_async_copy(token_tile(zero_ref, 0), token_tile(xs_ref, p0 + r), sem)

            lax.fori_loop(0, n, lambda r, c2: (zcopy(r).start(), c2)[1], 0)
            lax.fori_loop(0, n, lambda r, c2: (zcopy(r).wait(), c2)[1], 0)
            return c

        lax.fori_loop(0, N_EXPERTS, per_expert, 0)

        def tail_tile(m, c):
            cp = pltpu.make_async_copy(
                zero_ref, xs_ref.at[pl.ds(pl.multiple_of(m * tile_rows, tile_rows), tile_rows), :], sem)
            cp.start()
            cp.wait()
            return c

        lax.fori_loop(nt_ref[0], n_tiles, tail_tile, 0)


def _dispatch(hp, dest_flat, pad_start, pad_len, n_tiles_used, n_rows):
    t = hp.shape[0] // TOK_SUB
    tm = TOK_TILE
    smem = pl.BlockSpec(memory_space=pltpu.SMEM)
    return pl.pallas_call(
        _dispatch_kernel,
        out_shape=jax.ShapeDtypeStruct((n_rows * TOK_SUB, LANES), u32),
        grid=(t // tm,),
        in_specs=[
            pl.BlockSpec((tm * TOP_K,), lambda i: (i,), memory_space=pltpu.SMEM),
            smem, smem, smem,
            pl.BlockSpec((tm * TOK_SUB, LANES), lambda i: (i, 0)),
        ],
        out_specs=pl.BlockSpec(memory_space=pl.ANY),
        scratch_shapes=[pltpu.VMEM((EXP_TILE * TOK_SUB, LANES), u32), pltpu.SemaphoreType.DMA(())],
        compiler_params=_cparams(("arbitrary",), VMEM_LIMIT_MID),
        name="moe_dispatch",
    )(dest_flat, pad_start, pad_len, n_tiles_used, hp)


def _group_start(te_ref, m):
    return jnp.logical_or(m == 0, te_ref[m] != te_ref[jnp.maximum(m - 1, 0)])


def _gate_up_kernel(te_ref, nx_ref, rv_ref, nt_ref, xs_ref, w_hbm, bg_ref, bu_ref, h_ref,
                    wbuf, sems, grp_ref, *, layer):
    j = pl.program_id(0)
    m = pl.program_id(1)
    n_pass = pl.num_programs(0)
    fc = wbuf.shape[3]
    half = EXP_TILE // 2
    valid = m < nt_ref[0]

    def copies(e, jj, slot):
        col = pl.multiple_of(jj * fc, fc)
        return (pltpu.make_async_copy(w_hbm.at[layer, e, :, pl.ds(col, fc)], wbuf.at[slot, 0], sems.at[slot, 0]),
                pltpu.make_async_copy(w_hbm.at[layer, e, :, pl.ds(pl.multiple_of(D_FF + col, fc), fc)],
                                      wbuf.at[slot, 1], sems.at[slot, 1]))

    @pl.when(jnp.logical_and(_group_start(te_ref, m), valid))
    def _():
        @pl.when(jnp.logical_and(j == 0, m == 0))
        def _():
            grp_ref[0] = 0
            for c in copies(te_ref[0], 0, 0):
                c.start()

        g = grp_ref[0]
        slot = g & 1
        for c in copies(te_ref[m], j, slot):
            c.wait()
        nxt = nx_ref[m]

        @pl.when(nxt >= 0)
        def _():
            for c in copies(nxt, j, 1 - slot):
                c.start()

        @pl.when(jnp.logical_and(nxt < 0, j + 1 < n_pass))
        def _():
            for c in copies(te_ref[0], j + 1, 1 - slot):
                c.start()

        grp_ref[0] = g + 1

    cur = (grp_ref[0] + 1) & 1
    dn = (((1,), (0,)), ((), ()))

    def swiglu(x):
        g_lin = lax.dot_general(x, wbuf[cur, 0], dn, preferred_element_type=f32)
        u_lin = lax.dot_general(x, wbuf[cur, 1], dn, preferred_element_type=f32)
        gate = jnp.minimum(g_lin + bg_ref[...], SWIGLU_LIMIT)
        up = jnp.clip(u_lin + bu_ref[...], -SWIGLU_LIMIT, SWIGLU_LIMIT)
        glu = gate * jax.nn.sigmoid(gate * SWIGLU_ALPHA)
        return _pack_bf16_pairs((up + 1.0) * glu)

    @pl.when(rv_ref[m] > half)
    def _():
        h_ref[...] = swiglu(_load_token_tiles_bf16(xs_ref, EXP_TILE))

    @pl.when(jnp.logical_and(rv_ref[m] > 0, rv_ref[m] <= half))
    def _():
        h_ref[pl.ds(0, half), :] = swiglu(_load_token_tiles_bf16(xs_ref, half))
        h_ref[pl.ds(half, half), :] = jnp.zeros((half, h_ref.shape[1]), h_ref.dtype)

    @pl.when(rv_ref[m] == 0)
    def _():
        h_ref[...] = jnp.zeros_like(h_ref)


def _gate_up(xs, w_gate_up, b_gate_up4, layer, tile_e, next_e, rows_valid, n_tiles_used):
    p = xs.shape[0] // TOK_SUB
    d = D_MODEL
    fc = D_FF // 2
    n_pass = D_FF // fc
    tm = EXP_TILE
    grid_spec = pltpu.PrefetchScalarGridSpec(
        num_scalar_prefetch=4,
        grid=(n_pass, p // tm),
        in_specs=[
            pl.BlockSpec((tm * TOK_SUB, LANES), lambda j, m, te, nx, rv, nt: (m, 0)),
            pl.BlockSpec(memory_space=pl.ANY),
            pl.BlockSpec((None, None, 1, fc), lambda j, m, te, nx, rv, nt: (layer, te[m], 0, j)),
            pl.BlockSpec((None, None, 1, fc), lambda j, m, te, nx, rv, nt: (layer, te[m], 0, n_pass + j)),
        ],
        out_specs=pl.BlockSpec((tm, fc // 2), lambda j, m, te, nx, rv, nt: (m, j)),
        scratch_shapes=[
            pltpu.VMEM((2, 2, d, fc), f32),
            pltpu.SemaphoreType.DMA((2, 2)),
            pltpu.SMEM((1,), i32),
        ],
    )
    return pl.pallas_call(
        functools.partial(_gate_up_kernel, layer=layer),
        out_shape=jax.ShapeDtypeStruct((p, D_FF // 2), u32),
        grid_spec=grid_spec,
        compiler_params=_cparams(("arbitrary", "arbitrary"), VMEM_LIMIT_BIG),
        name="moe_gate_up",
    )(tile_e, next_e, rows_valid, n_tiles_used, xs, w_gate_up, b_gate_up4, b_gate_up4)


def _down_kernel(te_ref, nx_ref, rv_ref, nt_ref, h_ref, w_hbm, b_ref, y_ref, wbuf, sems, grp_ref, *, layer):
    m = pl.program_id(0)
    fc = D_FF // 2
    half = EXP_TILE // 2
    valid = m < nt_ref[0]

    def copy(e, slot):
        return pltpu.make_async_copy(w_hbm.at[layer, e], wbuf.at[slot], sems.at[slot])

    @pl.when(jnp.logical_and(_group_start(te_ref, m), valid))
    def _():
        @pl.when(m == 0)
        def _():
            grp_ref[0] = 0
            copy(te_ref[0], 0).start()

        g = grp_ref[0]
        slot = g & 1
        copy(te_ref[m], slot).wait()
        nxt = nx_ref[m]

        @pl.when(nxt >= 0)
        def _():
            copy(nxt, 1 - slot).start()

        grp_ref[0] = g + 1

    cur = (grp_ref[0] + 1) & 1

    def project(hp):
        h = jnp.concatenate([_unpack_bf16_pairs(hp[:, :fc // 2]), _unpack_bf16_pairs(hp[:, fc // 2:])], axis=1)
        y = lax.dot_general(h, wbuf[cur], (((1,), (0,)), ((), ())), preferred_element_type=f32)
        return _pack_bf16_pairs(y + b_ref[...])

    @pl.when(rv_ref[m] > half)
    def _():
        _store_token_tiles(y_ref, project(h_ref[...]))

    @pl.when(jnp.logical_and(rv_ref[m] > 0, rv_ref[m] <= half))
    def _():
        _store_token_tiles(y_ref, project(h_ref[pl.ds(0, half), :]))
        y_ref[pl.ds(half * TOK_SUB, half * TOK_SUB), :] = jnp.zeros((half * TOK_SUB, LANES), y_ref.dtype)

    @pl.when(rv_ref[m] == 0)
    def _():
        y_ref[...] = jnp.zeros_like(y_ref)


def _down(hs, w_down, b_down4, layer, tile_e, next_e, rows_valid, n_tiles_used):
    p, half = hs.shape
    d_ff = 2 * half
    d = w_down.shape[3]
    tm = EXP_TILE
    grid_spec = pltpu.PrefetchScalarGridSpec(
        num_scalar_prefetch=4,
        grid=(p // tm,),
        in_specs=[
            pl.BlockSpec((tm, half), lambda m, te, nx, rv, nt: (m, 0)),
            pl.BlockSpec(memory_space=pl.ANY),
            pl.BlockSpec((None, None, 1, d), lambda m, te, nx, rv, nt: (layer, te[m], 0, 0)),
        ],
        out_specs=pl.BlockSpec((tm * TOK_SUB, LANES), lambda m, te, nx, rv, nt: (m, 0)),
        scratch_shapes=[
            pltpu.VMEM((2, d_ff, d), f32),
            pltpu.SemaphoreType.DMA((2,)),
            pltpu.SMEM((1,), i32),
        ],
    )
    return pl.pallas_call(
        functools.partial(_down_kernel, layer=layer),
        out_shape=jax.ShapeDtypeStruct((p * TOK_SUB, LANES), u32),
        grid_spec=grid_spec,
        compiler_params=_cparams(("arbitrary",), VMEM_LIMIT_BIG),
        name="moe_down",
    )(tile_e, next_e, rows_valid, n_tiles_used, hs, w_down, b_down4)


def _combine_kernel(dest_ref, dest_next_ref, ys_ref, tw_ref, x1_ref, gf_ref, lng_ref, lnb_ref, o_ref,
                    buf_ref, sems):
    tm = x1_ref.shape[0]
    i = pl.program_id(0)
    n = pl.num_programs(0)
    slot = i & 1

    def token_tile(ref, r):
        return ref.at[pl.ds(pl.multiple_of(r * TOK_SUB, TOK_SUB), TOK_SUB), :]

    def copy(dref, s, tok, k):
        return pltpu.make_async_copy(token_tile(ys_ref, dref[tok * TOP_K + k]),
                                     token_tile(buf_ref, (s * TOP_K + k) * tm + tok), sems.at[s])

    def start_all(dref, s):
        def body(tok, c):
            for k in range(TOP_K):
                copy(dref, s, tok, k).start(priority=k % 2)
            return c
        lax.fori_loop(0, tm, body, 0, unroll=2)

    @pl.when(i == 0)
    def _():
        start_all(dest_ref, 0)

    @pl.when(i + 1 < n)
    def _():
        start_all(dest_next_ref, 1 - slot)

    def wait_body(tok, c):
        for k in range(TOP_K):
            copy(dest_ref, slot, tok, k).wait()
        return c

    lax.fori_loop(0, tm, wait_body, 0, unroll=2)

    tw = tw_ref[...]
    y = jnp.zeros((tm, D_MODEL), f32)
    for k in range(TOP_K):
        y = y + tw[:, k:k + 1] * _load_token_tiles_bf16(buf_ref, tm, row0=(slot * TOP_K + k) * tm).astype(f32)
    o_ref[...] = _layer_norm(DEEPNORM_ALPHA * x1_ref[...] + gf_ref[...] * y, lng_ref[...], lnb_ref[...])


def _combine(ys, dest_flat, top_w, x1, mod3, ln_g, ln_b, seq):
    t, d = x1.shape
    tm = TOK_TILE
    tps = seq // tm
    n_steps = t // tm
    grid_spec = pltpu.PrefetchScalarGridSpec(
        num_scalar_prefetch=0,
        grid=(n_steps,),
        in_specs=[
            pl.BlockSpec((tm * TOP_K,), lambda i: (i,), memory_space=pltpu.SMEM),
            pl.BlockSpec((tm * TOP_K,), lambda i: (jnp.minimum(i + 1, n_steps - 1),), memory_space=pltpu.SMEM),
            pl.BlockSpec(memory_space=pl.ANY),
            pl.BlockSpec((tm, TOP_K), lambda i: (i, 0)),
            pl.BlockSpec((tm, d), lambda i: (i, 0)),
            pl.BlockSpec((None, 1, d), lambda i: (i // tps, 0, 5)),
            pl.BlockSpec((1, d), lambda i: (0, 0)),
            pl.BlockSpec((1, d), lambda i: (0, 0)),
        ],
        out_specs=pl.BlockSpec((tm, d), lambda i: (i, 0)),
        scratch_shapes=[pltpu.VMEM((2 * TOP_K * tm * TOK_SUB, LANES), u32), pltpu.SemaphoreType.DMA((2,))],
    )
    return pl.pallas_call(
        _combine_kernel,
        out_shape=jax.ShapeDtypeStruct((t, d), f32),
        grid_spec=grid_spec,
        compiler_params=_cparams(("arbitrary",), VMEM_LIMIT_MID),
        name="moe_combine",
    )(dest_flat, dest_flat, ys, top_w, x1, mod3, ln_g.reshape(1, -1), ln_b.reshape(1, -1))


def _routing_tables(top_e, rank, counts, n_tiles):
    tm = EXP_TILE
    counts = counts.reshape(-1)
    padded = (counts + tm - 1) // tm * tm
    pend = jnp.cumsum(padded)
    pstart = pend - padded
    dest = (pstart[top_e] + rank).reshape(-1).astype(i32)
    tile_start = jnp.arange(n_tiles, dtype=i32) * tm
    tile_e = jnp.sum((pend[None, :] <= tile_start[:, None]).astype(i32), axis=1)
    tile_e = jnp.minimum(tile_e, N_EXPERTS - 1).astype(i32)
    n_used = (pend[-1] // tm).astype(i32).reshape(1)
    mine = jnp.arange(N_EXPERTS, dtype=i32)[None, :] == tile_e[:, None]
    group_end = jnp.sum(jnp.where(mine, pend[None, :], 0), axis=1)
    data_end = jnp.sum(jnp.where(mine, (pstart + counts)[None, :], 0), axis=1)
    rows_valid = jnp.where(tile_start < pend[-1], jnp.clip(data_end - tile_start, 0, tm), 0).astype(i32)
    next_e = jnp.sum((pend[None, :] <= group_end[:, None]).astype(i32), axis=1)
    next_e = jnp.where(group_end < pend[-1], jnp.minimum(next_e, N_EXPERTS - 1), -1).astype(i32)
    return dest, tile_e, next_e, rows_valid, n_used, (pstart + counts).astype(i32), (padded - counts).astype(i32)


def kernel(x, c, rel_bias, w_ada, b_ada, w_in, conv_w, conv_b, lru_w_a, lru_b_a, lru_w_x, lru_b_x, lru_lambda, norm_lru_g, norm_attn_g, w_out, ln_mix_g, ln_mix_b, w_router, b_router, w_gate_up, b_gate_up, w_down, b_down, ln_ffn_g, ln_ffn_b):
    nb, seq, d = x.shape
    t = nb * seq
    mod = _ada_mod(c, w_ada, b_ada)
    b1, b2, b3 = _attn_bias_tables(rel_bias)
    n_tiles = t * TOP_K // EXP_TILE + N_EXPERTS
    q_lo = 2 * D_LRU
    x2 = x.reshape(t, d)
    for l in range(DEPTH):
        mod3 = mod[l].reshape(nb, 1, 6 * d)
        w_in_l = w_in[l].at[:, q_lo:q_lo + D_ATTN].multiply(HEAD_DIM ** -0.5).astype(bf16)
        proj = _in_proj(x2, mod3, w_in_l, seq).reshape(nb, seq, D_IN)
        wg = jnp.concatenate([lru_w_a[l], lru_w_x[l]], axis=-1).astype(bf16)
        bg = jnp.concatenate([lru_b_a[l].reshape(2, N_LRU_BLOCKS, 1, LRU_BLOCK),
                              lru_b_x[l].reshape(2, N_LRU_BLOCKS, 1, LRU_BLOCK)], axis=-1)
        lru = _lru_group(proj, conv_w[l], conv_b[l], wg, bg, lru_lambda[l])
        att = _dilated_attention(proj, b1, b2, b3)
        x1, hp, top_e, top_w, rank, counts = _out_proj_router(
            lru.reshape(t, D_LRU), att.reshape(t, D_ATTN), x2, norm_lru_g[l], norm_attn_g[l],
            w_out[l].astype(bf16), mod3, ln_mix_g[l], ln_mix_b[l], w_router[l], b_router[l], seq)
        dest, tile_e, next_e, rows_valid, n_used, pad_start, pad_len = _routing_tables(top_e, rank, counts, n_tiles)
        xs = _dispatch(hp, dest, pad_start, pad_len, n_used, n_tiles * EXP_TILE)
        hs = _gate_up(xs, w_gate_up, b_gate_up.reshape(DEPTH, N_EXPERTS, 1, 2 * D_FF), l,
                      tile_e, next_e, rows_valid, n_used)
        ys = _down(hs, w_down, b_down.reshape(DEPTH, N_EXPERTS, 1, d), l, tile_e, next_e, rows_valid, n_used)
        x2 = _combine(ys, dest, top_w, x1, mod3, ln_ffn_g[l], ln_ffn_b[l], seq)
    return x2.reshape(nb, seq, d)
```

```python
import functools
import math

import jax
import jax.numpy as jnp
from jax import lax
from jax.experimental import pallas as pl
from jax.experimental.pallas import tpu as pltpu

f32 = jnp.float32
bf16 = jnp.bfloat16
i32 = jnp.int32
u32 = jnp.uint32

D_MODEL = 2048
DEPTH = 2
D_LRU = 1024
N_LRU_BLOCKS = 8
LRU_BLOCK = 128
CONV_WIDTH = 4
LRU_C = 8.0
D_ATTN = 1024
HEAD_DIM = 128
N_HEADS = 8
D_IN = 2 * D_LRU + 3 * D_ATTN
NEG_INF = -1e30
N_BUCKETS = 32
MAX_EXACT = 8
MAX_DISTANCE = 1024
N_EXPERTS = 32
TOP_K = 4
D_FF = D_MODEL
SWIGLU_LIMIT = 7.0
SWIGLU_ALPHA = 1.702
DEEPNORM_ALPHA = (2 * DEPTH) ** 0.25
LN_EPS = 1e-5

VMEM_LIMIT_BIG = 56 * 1024 * 1024
VMEM_LIMIT_MID = 40 * 1024 * 1024

ROW_TILE = 512
ROUTER_TILE = 512
TOK_TILE = 256
EXP_TILE = 512
LRU_CHUNK = 128
Q_BLOCK = 128
NAT_BLOCK = 256
SUB_ROWS = 128
HALF = 64
RES16 = 16
STAGE_PITCH = 24


def _cparams(sem, vmem=None):
    return pltpu.CompilerParams(dimension_semantics=sem, vmem_limit_bytes=vmem)


def _ada_kernel(c_ref, w_ref, b_ref, o_ref):
    c = c_ref[...]
    ca = c * jax.nn.sigmoid(c)
    o_ref[0] = jnp.dot(ca, w_ref[0], precision=lax.Precision.HIGHEST,
                       preferred_element_type=f32) + b_ref[0]


def _ada_mod(c, w_ada, b_ada):
    depth, d, n = w_ada.shape
    b = c.shape[0]
    tn = 1024
    return pl.pallas_call(
        _ada_kernel,
        out_shape=jax.ShapeDtypeStruct((depth, b, n), f32),
        grid=(depth, n // tn),
        in_specs=[
            pl.BlockSpec((b, d), lambda l, j: (0, 0)),
            pl.BlockSpec((1, d, tn), lambda l, j: (l, 0, j)),
            pl.BlockSpec((1, 1, tn), lambda l, j: (l, 0, j)),
        ],
        out_specs=pl.BlockSpec((1, b, tn), lambda l, j: (l, 0, j)),
        compiler_params=_cparams(("parallel", "parallel"), VMEM_LIMIT_MID),
        name="ada_mod",
    )(c, w_ada, b_ada.reshape(depth, 1, n))


def _inproj_kernel(x_ref, sh_ref, sc_ref, w_ref, o_ref):
    h = x_ref[...] * (1.0 + sc_ref[...]) + sh_ref[...]
    o_ref[...] = jnp.dot(h.astype(bf16), w_ref[...], preferred_element_type=f32).astype(o_ref.dtype)


def _in_proj(x2, mod3, w_in_bf, seq):
    t, d = x2.shape
    n = w_in_bf.shape[1]
    tn = n // 2
    tiles_per_seq = seq // ROW_TILE
    return pl.pallas_call(
        _inproj_kernel,
        out_shape=jax.ShapeDtypeStruct((t, n), bf16),
        grid=(n // tn, t // ROW_TILE),
        in_specs=[
            pl.BlockSpec((ROW_TILE, d), lambda j, i: (i, 0)),
            pl.BlockSpec((None, 1, d), lambda j, i: (i // tiles_per_seq, 0, 0)),
            pl.BlockSpec((None, 1, d), lambda j, i: (i // tiles_per_seq, 0, 1)),
            pl.BlockSpec((d, tn), lambda j, i: (0, j)),
        ],
        out_specs=pl.BlockSpec((ROW_TILE, tn), lambda j, i: (i, j)),
        compiler_params=_cparams(("parallel", "parallel"), VMEM_LIMIT_BIG),
        name="in_proj",
    )(x2, mod3, mod3, w_in_bf)


def _lru_kernel(xb_ref, gb_ref, cw_ref, cb_ref, wg_ref, bg_ref, lam_ref, o_ref,
                xt_ref, hs_ref, a_ref, b_ref, hb_ref):
    nb, seq, _ = xb_ref.shape
    tc = LRU_CHUNK
    rows = tc * nb
    n_chunks = seq // tc
    front = (CONV_WIDTH // 2) * nb

    xt_ref[pl.ds(0, front), :] = jnp.zeros((front, LRU_BLOCK), f32)
    xt_ref[pl.ds(front + seq * nb, nb), :] = jnp.zeros((nb, LRU_BLOCK), f32)
    for b in range(nb):
        xt_ref[pl.ds(front + b, seq, stride=nb), :] = xb_ref[b].astype(f32)

    lam = lam_ref[...]
    neg_c_sp = -LRU_C * jax.nn.softplus(-lam)
    cw = cw_ref[...]
    cb = cb_ref[...]

    def gates(d, r0):
        xc = cb
        for j in range(CONV_WIDTH):
            xc = xc + cw[j:j + 1, :] * xt_ref[pl.ds(pl.multiple_of(r0 + j * nb, nb), rows), :]
        g = jnp.dot(xc.astype(bf16), wg_ref[d], preferred_element_type=f32) + bg_ref[d]
        r = 0.5 + 0.5 * jnp.tanh(0.5 * g[:, :LRU_BLOCK])
        i = 0.5 + 0.5 * jnp.tanh(0.5 * g[:, LRU_BLOCK:])
        log_a = neg_c_sp[d:d + 1, :] * r
        a = jnp.exp(log_a)
        u = a * a
        z = 2.0 * log_a
        one_minus = jnp.where(u == 1.0, -z, (1.0 - u) * z / jnp.log(u))
        root = jnp.where(one_minus > 0.0, one_minus * lax.rsqrt(one_minus), 0.0)
        a_ref[...] = a
        b_ref[...] = root * (i * xc)

    def fwd_chunk(c, h):
        r0 = pl.multiple_of(c * rows, rows)
        gates(0, r0)

        def step(t, h):
            rr = pl.multiple_of(t * nb, nb)
            h = a_ref[pl.ds(rr, nb), :] * h + b_ref[pl.ds(rr, nb), :]
            hs_ref[pl.ds(r0 + rr, nb), :] = h
            return h

        return lax.fori_loop(0, tc, step, h, unroll=8)

    lax.fori_loop(0, n_chunks, fwd_chunk, jnp.zeros((nb, LRU_BLOCK), f32))

    def bwd_chunk(ci, h):
        c = n_chunks - 1 - ci
        r0 = pl.multiple_of(c * rows, rows)
        gates(1, r0)

        def step(ti, h):
            rr = pl.multiple_of((tc - 1 - ti) * nb, nb)
            h = a_ref[pl.ds(rr, nb), :] * h + b_ref[pl.ds(rr, nb), :]
            hb_ref[pl.ds(rr, nb), :] = h
            return h

        h = lax.fori_loop(0, tc, step, h, unroll=8)
        hs_ref[pl.ds(r0, rows), :] = hs_ref[pl.ds(r0, rows), :] + hb_ref[...]
        t0 = pl.multiple_of(c * tc, tc)
        for b in range(nb):
            hr = hs_ref[pl.ds(r0 + b, tc, stride=nb), :]
            gate = jax.nn.gelu(gb_ref[b, pl.ds(t0, tc), :].astype(f32))
            o_ref[b, pl.ds(t0, tc), :] = (hr * gate).astype(o_ref.dtype)
        return h

    lax.fori_loop(0, n_chunks, bwd_chunk, jnp.zeros((nb, LRU_BLOCK), f32))


def _lru_group(proj3, conv_w, conv_b, wg, bg, lam):
    nb, seq, _ = proj3.shape
    g = N_LRU_BLOCKS
    return pl.pallas_call(
        _lru_kernel,
        out_shape=jax.ShapeDtypeStruct((nb, seq, D_LRU), bf16),
        grid=(g,),
        in_specs=[
            pl.BlockSpec((nb, seq, LRU_BLOCK), lambda j: (0, 0, j)),
            pl.BlockSpec((nb, seq, LRU_BLOCK), lambda j: (0, 0, g + j)),
            pl.BlockSpec((CONV_WIDTH, LRU_BLOCK), lambda j: (0, j)),
            pl.BlockSpec((1, LRU_BLOCK), lambda j: (0, j)),
            pl.BlockSpec((2, None, LRU_BLOCK, 2 * LRU_BLOCK), lambda j: (0, j, 0, 0)),
            pl.BlockSpec((2, None, 1, 2 * LRU_BLOCK), lambda j: (0, j, 0, 0)),
            pl.BlockSpec((2, LRU_BLOCK), lambda j: (0, j)),
        ],
        out_specs=pl.BlockSpec((nb, seq, LRU_BLOCK), lambda j: (0, 0, j)),
        scratch_shapes=[
            pltpu.VMEM(((seq + CONV_WIDTH - 1) * nb, LRU_BLOCK), f32),
            pltpu.VMEM((seq * nb, LRU_BLOCK), f32),
            pltpu.VMEM((LRU_CHUNK * nb, LRU_BLOCK), f32),
            pltpu.VMEM((LRU_CHUNK * nb, LRU_BLOCK), f32),
            pltpu.VMEM((LRU_CHUNK * nb, LRU_BLOCK), f32),
        ],
        compiler_params=_cparams(("parallel",), VMEM_LIMIT_BIG),
        name="rg_lru",
    )(proj3, proj3, conv_w, conv_b.reshape(1, D_LRU), wg, bg, lam)


def _t5_bucket(rel):
    nbk = N_BUCKETS // 2
    ret = jnp.where(rel > 0, nbk, 0)
    n = jnp.abs(rel)
    nf = jnp.maximum(n, 1).astype(f32)
    large = MAX_EXACT + (jnp.log(nf / MAX_EXACT) / math.log(MAX_DISTANCE / MAX_EXACT)
                         * (nbk - MAX_EXACT)).astype(i32)
    large = jnp.minimum(large, nbk - 1)
    return ret + jnp.where(n < MAX_EXACT, n, large)


def _attn_bias_tables(rel_bias):
    qb = Q_BLOCK
    rb = rel_bias.astype(f32)
    i = jnp.arange(qb, dtype=i32)[:, None]

    def tile(step, dilation, valid):
        onehot = jax.nn.one_hot(_t5_bucket(step * dilation), N_BUCKETS, dtype=f32)
        t = jnp.einsum("qkn,nh->hqk", onehot, rb, precision=lax.Precision.HIGHEST)
        return jnp.where(valid[None], t, NEG_INF)

    i1 = jnp.arange(NAT_BLOCK, dtype=i32)[:, None]
    win = NAT_BLOCK + 2 * HALF
    j = jnp.arange(win, dtype=i32)[None, :]
    step1 = j - HALF - i1
    band1 = jnp.abs(step1) <= HALF
    b1 = jnp.stack([tile(step1, 1, band1),
                    tile(step1, 1, band1 & (j >= HALF)),
                    tile(step1, 1, band1 & (j < win - HALF))],
                   axis=1)
    lk = jnp.arange(qb, dtype=i32)[None, :]
    step2 = jnp.concatenate(
        [jnp.concatenate([4 * (lk - i) + (jk - jq) for jk in range(4)], axis=1) for jq in range(4)], axis=0)
    b2 = tile(step2, 4, jnp.abs(step2) <= HALF)
    step3 = lk - i
    b3 = tile(step3, 16, jnp.abs(step3) <= HALF)
    return b1, b2, b3


def _lane_tiles(a):
    return [a[:, c:c + LANES] for c in range(0, a.shape[1], LANES)] if a.shape[1] % LANES == 0 else [a]


def _softmax_parts(s):
    m = jnp.max(functools.reduce(jnp.maximum, _lane_tiles(s)), axis=-1, keepdims=True)
    p = jnp.exp(s - m)
    return m, p, jnp.sum(functools.reduce(jnp.add, _lane_tiles(p)), axis=-1, keepdims=True)


def _attn_kernel(q_ref, k_ref, v_ref, b1_ref, b2_ref, b3_ref, o_ref,
                 stage_ref, qp_ref, kp_ref, vp_ref, kpad_ref, vpad_ref, o23_ref, l23_ref):
    seq = q_ref.shape[0]
    qb = NAT_BLOCK
    nblk = seq // qb
    per_res = seq // RES16
    cls = 4 * per_res
    dn = (((1,), (1,)), ((), ()))

    def perm_row0(r16):
        return ((r16 % 4) * 4 + r16 // 4) * per_res

    def pitched(ref, g, rows=RES16):
        return ref.at[pl.ds(pl.multiple_of(g * STAGE_PITCH, 8), rows), :]

    for src, dst in ((q_ref, qp_ref), (k_ref, kp_ref), (v_ref, vp_ref)):
        def fill(g, c, src=src):
            pitched(stage_ref, g)[...] = src[pl.ds(pl.multiple_of(g * RES16, RES16), RES16), :].astype(f32)
            return c

        lax.fori_loop(0, per_res, fill, 0, unroll=8)
        for r in range(RES16):
            dst[pl.ds(perm_row0(r), per_res), :] = stage_ref[pl.ds(r, per_res, stride=STAGE_PITCH), :].astype(bf16)

    zpad = jnp.zeros((HALF, HEAD_DIM), bf16)
    kpad_ref[pl.ds(0, HALF), :] = zpad
    vpad_ref[pl.ds(0, HALF), :] = zpad
    kpad_ref[pl.ds(HALF + seq, HALF), :] = zpad
    vpad_ref[pl.ds(HALF + seq, HALF), :] = zpad
    kpad_ref[pl.ds(HALF, seq), :] = k_ref[...]
    vpad_ref[pl.ds(HALF, seq), :] = v_ref[...]

    for r4 in range(4):
        base = r4 * cls
        q4 = qp_ref[pl.ds(base, cls), :]
        k4 = kp_ref[pl.ds(base, cls), :]
        v4 = vp_ref[pl.ds(base, cls), :]
        raw = lax.dot_general(q4, k4, dn, preferred_element_type=f32)
        m2, p2, d2 = _softmax_parts(raw + b2_ref[...])
        o2 = jnp.dot(p2.astype(bf16), v4, preferred_element_type=f32)
        m3s, d3s, o3s = [], [], []
        for jq in range(4):
            lo, hi = jq * per_res, (jq + 1) * per_res
            m3, p3, d3 = _softmax_parts(raw[lo:hi, lo:hi] + b3_ref[...])
            m3s.append(m3)
            d3s.append(d3)
            o3s.append(jnp.dot(p3.astype(bf16), v4[lo:hi], preferred_element_type=f32))
        m3 = jnp.concatenate(m3s, axis=0)
        d3 = jnp.concatenate(d3s, axis=0)
        o3 = jnp.concatenate(o3s, axis=0)
        m23 = jnp.maximum(m2, m3)
        w2 = jnp.exp(m2 - m23)
        w3 = jnp.exp(m3 - m23)
        d23 = w2 * d2 + w3 * d3
        o23_ref[pl.ds(base, cls), :] = (w2 * o2 + w3 * o3) / d23
        l23_ref[pl.ds(base, cls), :] = jnp.broadcast_to(m23 + jnp.log(d23), (cls, HEAD_DIM))

    for r in range(RES16):
        stage_ref[pl.ds(r, per_res, stride=STAGE_PITCH), :] = o23_ref[pl.ds(perm_row0(r), per_res), :]
    for r in range(RES16):
        o23_ref[pl.ds(r, per_res, stride=STAGE_PITCH), :] = l23_ref[pl.ds(perm_row0(r), per_res), :]

    def natural_rows(ref, g0, rows):
        return jnp.concatenate([pitched(ref, g0 + i)[...] for i in range(rows // RES16)], axis=0)

    per_group = 4
    n_groups = nblk // per_group

    def nat_group(gi, carry):
        for sb in range(per_group):
            row0 = pl.multiple_of((gi * per_group + sb) * qb, qb)
            q = q_ref[pl.ds(row0, qb), :]
            kw = kpad_ref[pl.ds(row0, qb + 2 * HALF), :]
            vw = vpad_ref[pl.ds(row0, qb + 2 * HALF), :]
            if sb == 0:
                bias = b1_ref[jnp.where(gi == 0, 1, 0)]
            elif sb == per_group - 1:
                bias = b1_ref[jnp.where(gi == n_groups - 1, 2, 0)]
            else:
                bias = b1_ref[0]
            s1 = lax.dot_general(q, kw, dn, preferred_element_type=f32) + bias
            m1, p1, d1 = _softmax_parts(s1)
            o1 = jnp.dot(p1.astype(bf16), vw, preferred_element_type=f32)
            g0 = (gi * per_group + sb) * (qb // RES16)
            l23 = natural_rows(o23_ref, g0, qb)
            on23 = natural_rows(stage_ref, g0, qb)
            mm = jnp.maximum(m1, l23)
            w1 = jnp.exp(m1 - mm)
            w23 = jnp.exp(l23 - mm)
            o_ref[pl.ds(row0, qb), :] = ((w1 * o1 + w23 * on23) / (w1 * d1 + w23)).astype(o_ref.dtype)
        return carry

    lax.fori_loop(0, n_groups, nat_group, 0)


def _dilated_attention(proj3, b1, b2, b3):
    nb, seq, _ = proj3.shape
    assert seq // RES16 == Q_BLOCK and seq % (4 * NAT_BLOCK) == 0
    q0 = 2 * D_LRU // HEAD_DIM
    k0 = q0 + N_HEADS
    v0 = k0 + N_HEADS
    return pl.pallas_call(
        _attn_kernel,
        out_shape=jax.ShapeDtypeStruct((nb, seq, D_ATTN), bf16),
        grid=(N_HEADS, nb),
        in_specs=[
            pl.BlockSpec((None, seq, HEAD_DIM), lambda h, b: (b, 0, q0 + h)),
            pl.BlockSpec((None, seq, HEAD_DIM), lambda h, b: (b, 0, k0 + h)),
            pl.BlockSpec((None, seq, HEAD_DIM), lambda h, b: (b, 0, v0 + h)),
            pl.BlockSpec((None, 3, NAT_BLOCK, NAT_BLOCK + 2 * HALF), lambda h, b: (h, 0, 0, 0)),
            pl.BlockSpec((None, 4 * Q_BLOCK, 4 * Q_BLOCK), lambda h, b: (h, 0, 0)),
            pl.BlockSpec((None, Q_BLOCK, Q_BLOCK), lambda h, b: (h, 0, 0)),
        ],
        out_specs=pl.BlockSpec((None, seq, HEAD_DIM), lambda h, b: (b, 0, h)),
        scratch_shapes=[
            pltpu.VMEM((seq // RES16 * STAGE_PITCH, HEAD_DIM), f32),
            pltpu.VMEM((seq, HEAD_DIM), bf16),
            pltpu.VMEM((seq, HEAD_DIM), bf16),
            pltpu.VMEM((seq, HEAD_DIM), bf16),
            pltpu.VMEM((seq + 2 * HALF, HEAD_DIM), bf16),
            pltpu.VMEM((seq + 2 * HALF, HEAD_DIM), bf16),
            pltpu.VMEM((seq // RES16 * STAGE_PITCH, HEAD_DIM), f32),
            pltpu.VMEM((seq, HEAD_DIM), f32),
        ],
        compiler_params=_cparams(("parallel", "parallel"), VMEM_LIMIT_MID),
        name="dilated_attn",
    )(proj3, proj3, proj3, b1, b2, b3)


def _layer_norm(z, g, b):
    mu = jnp.mean(z, axis=-1, keepdims=True)
    zc = z - mu
    var = jnp.mean(zc * zc, axis=-1, keepdims=True)
    return zc * lax.rsqrt(var + LN_EPS) * g + b


def _rms_norm(v, g):
    return v * lax.rsqrt(jnp.mean(v * v, axis=-1, keepdims=True) + LN_EPS) * g


def _pack_bf16_pairs(v):
    n = v.shape[1] // 2
    lo = pltpu.bitcast(v[:, :n].astype(bf16).astype(f32), u32)
    hi = pltpu.bitcast(v[:, n:].astype(bf16).astype(f32), u32)
    return (lo >> 16) | (hi & jnp.uint32(0xFFFF0000))


def _unpack_bf16_pairs(p):
    lo = pltpu.bitcast(p << 16, f32)
    hi = pltpu.bitcast(p & jnp.uint32(0xFFFF0000), f32)
    return jnp.concatenate([lo.astype(bf16), hi.astype(bf16)], axis=1)


LANES = 128
TOK_SUB = D_MODEL // 2 // LANES


def _store_token_tiles(ref, packed, row0=0):
    rows = packed.shape[0]
    for c in range(TOK_SUB):
        ref[pl.ds(TOK_SUB * row0 + c, rows, stride=TOK_SUB), :] = packed[:, c * LANES:(c + 1) * LANES]


def _load_token_tiles_bf16(ref, rows, row0=0):
    los, his = [], []
    for c in range(TOK_SUB):
        p = ref[pl.ds(TOK_SUB * row0 + c, rows, stride=TOK_SUB), :]
        los.append(pltpu.bitcast(p << 16, f32).astype(bf16))
        his.append(pltpu.bitcast(p & jnp.uint32(0xFFFF0000), f32).astype(bf16))
    return jnp.concatenate(los + his, axis=1)


def _outproj_kernel(lru_ref, att_ref, x_ref, gl_ref, ga_ref, w_ref, gm_ref, shf_ref, scf_ref,
                    lng_ref, lnb_ref, wrc_ref, br_ref,
                    x1_ref, hp_ref, te_ref, tw_ref, rk_ref, cnt_ref, carry_ref):
    tm = x_ref.shape[0]
    step = pl.program_id(0)

    @pl.when(step == 0)
    def _():
        carry_ref[...] = jnp.zeros_like(carry_ref)

    ln = _rms_norm(lru_ref[...].astype(f32), gl_ref[...]).astype(bf16)
    an = _rms_norm(att_ref[...].astype(f32), ga_ref[...]).astype(bf16)
    y = (jnp.dot(ln, w_ref[pl.ds(0, D_LRU), :], preferred_element_type=f32)
         + jnp.dot(an, w_ref[pl.ds(D_LRU, D_ATTN), :], preferred_element_type=f32))
    x1 = _layer_norm(DEEPNORM_ALPHA * x_ref[...] + gm_ref[...] * y, lng_ref[...], lnb_ref[...])
    x1_ref[...] = x1
    hf = x1 * (1.0 + scf_ref[...]) + shf_ref[...]
    _store_token_tiles(hp_ref, _pack_bf16_pairs(hf))

    hf_hi = hf.astype(bf16)
    hf_lo = (hf - hf_hi.astype(f32)).astype(bf16)
    hi_both = jnp.dot(hf_hi, wrc_ref[...], preferred_element_type=f32)
    lo_hi = jnp.dot(hf_lo, wrc_ref[:, :N_EXPERTS], preferred_element_type=f32)
    logits = hi_both[:, :N_EXPERTS] + (lo_hi + hi_both[:, N_EXPERTS:]) + br_ref[...]
    lane = lax.broadcasted_iota(i32, (tm, N_EXPERTS), 1)
    col4 = lax.broadcasted_iota(i32, (tm, TOP_K), 1)
    work = logits
    vals, idxs = [], []
    for _k in range(TOP_K):
        m = jnp.max(work, axis=-1, keepdims=True)
        idx = jnp.min(jnp.where(work == m, lane, N_EXPERTS), axis=-1, keepdims=True)
        vals.append(m)
        idxs.append(idx)
        work = jnp.where(lane == idx, -jnp.inf, work)
    exps = [jnp.exp(v - vals[0]) for v in vals]
    den = exps[0] + exps[1] + exps[2] + exps[3]

    onehot = jnp.zeros((tm, N_EXPERTS), f32)
    for idx in idxs:
        onehot = onehot + (lane == idx).astype(f32)
    ri = lax.broadcasted_iota(i32, (tm, tm), 0)
    ci = lax.broadcasted_iota(i32, (tm, tm), 1)
    lower = (ri > ci).astype(bf16)
    before = jnp.dot(lower, onehot.astype(bf16), preferred_element_type=f32) + carry_ref[...]
    te = jnp.zeros((tm, TOP_K), i32)
    tw = jnp.zeros((tm, TOP_K), f32)
    rk = jnp.zeros((tm, TOP_K), i32)
    for k in range(TOP_K):
        rank_k = jnp.sum(jnp.where(lane == idxs[k], before, 0.0), axis=-1, keepdims=True)
        te = jnp.where(col4 == k, idxs[k], te)
        tw = jnp.where(col4 == k, exps[k] / den, tw)
        rk = jnp.where(col4 == k, rank_k.astype(i32), rk)
    te_ref[...] = te
    tw_ref[...] = tw
    rk_ref[...] = rk
    carry_ref[...] = carry_ref[...] + jnp.sum(onehot, axis=0, keepdims=True)
    cnt_ref[...] = carry_ref[...].astype(i32)


def _out_proj_router(lru2, att2, x2, g_lru, g_attn, w_out_bf, mod3, ln_g, ln_b, w_router, b_router, seq):
    t, d = x2.shape
    tm = ROUTER_TILE
    tps = seq // tm
    row = lambda i: (i, 0)
    const = lambda i: (0, 0)
    wr_hi = w_router.astype(bf16)
    wr_cat = jnp.concatenate([wr_hi, (w_router - wr_hi.astype(f32)).astype(bf16)], axis=1)
    return pl.pallas_call(
        _outproj_kernel,
        out_shape=(
            jax.ShapeDtypeStruct((t, d), f32),
            jax.ShapeDtypeStruct((t * TOK_SUB, LANES), u32),
            jax.ShapeDtypeStruct((t, TOP_K), i32),
            jax.ShapeDtypeStruct((t, TOP_K), f32),
            jax.ShapeDtypeStruct((t, TOP_K), i32),
            jax.ShapeDtypeStruct((1, N_EXPERTS), i32),
        ),
        grid=(t // tm,),
        in_specs=[
            pl.BlockSpec((tm, D_LRU), row),
            pl.BlockSpec((tm, D_ATTN), row),
            pl.BlockSpec((tm, d), row),
            pl.BlockSpec((1, D_LRU), const),
            pl.BlockSpec((1, D_ATTN), const),
            pl.BlockSpec((d, d), const),
            pl.BlockSpec((None, 1, d), lambda i: (i // tps, 0, 2)),
            pl.BlockSpec((None, 1, d), lambda i: (i // tps, 0, 3)),
            pl.BlockSpec((None, 1, d), lambda i: (i // tps, 0, 4)),
            pl.BlockSpec((1, d), const),
            pl.BlockSpec((1, d), const),
            pl.BlockSpec((d, 2 * N_EXPERTS), const),
            pl.BlockSpec((1, N_EXPERTS), const),
        ],
        out_specs=(
            pl.BlockSpec((tm, d), row),
            pl.BlockSpec((tm * TOK_SUB, LANES), row),
            pl.BlockSpec((tm, TOP_K), row),
            pl.BlockSpec((tm, TOP_K), row),
            pl.BlockSpec((tm, TOP_K), row),
            pl.BlockSpec((1, N_EXPERTS), const),
        ),
        scratch_shapes=[pltpu.VMEM((1, N_EXPERTS), f32)],
        compiler_params=_cparams(("arbitrary",), VMEM_LIMIT_BIG),
        name="out_proj_router",
    )(lru2, att2, x2, g_lru.reshape(1, -1), g_attn.reshape(1, -1), w_out_bf, mod3, mod3, mod3,
      ln_g.reshape(1, -1), ln_b.reshape(1, -1), wr_cat, b_router.reshape(1, -1))


def _dispatch_kernel(dest_ref, pad_start_ref, pad_len_ref, nt_ref, hp_ref, xs_ref, zero_ref, sem):
    tm = hp_ref.shape[0] // TOK_SUB
    tile_rows = EXP_TILE * TOK_SUB
    n_tiles = xs_ref.shape[0] // tile_rows

    def token_tile(ref, r):
        return ref.at[pl.ds(pl.multiple_of(r * TOK_SUB, TOK_SUB), TOK_SUB), :]

    def copy(tok, k):
        return pltpu.make_async_copy(token_tile(hp_ref, tok), token_tile(xs_ref, dest_ref[tok * TOP_K + k]), sem)

    def start(tok, c):
        for k in range(TOP_K):
            copy(tok, k).start(priority=k % 2)
        return c

    def wait(tok, c):
        for k in range(TOP_K):
            copy(tok, k).wait()
        return c

    lax.fori_loop(0, tm, start, 0, unroll=2)
    lax.fori_loop(0, tm, wait, 0, unroll=2)

    @pl.when(pl.program_id(0) == pl.num_programs(0) - 1)
    def _():
        zero_ref[...] = jnp.zeros_like(zero_ref)

        def per_expert(e, c):
            p0 = pad_start_ref[e]
            n = pad_len_ref[e]

            def zcopy(r):
                return pltpu.make_async_copy(token_tile(zero_ref, 0), token_tile(xs_ref, p0 + r), sem)

            lax.fori_loop(0, n, lambda r, c2: (zcopy(r).start(), c2)[1], 0)
            lax.fori_loop(0, n, lambda r, c2: (zcopy(r).wait(), c2)[1], 0)
            return c

        lax.fori_loop(0, N_EXPERTS, per_expert, 0)

        def tail_tile(m, c):
            cp = pltpu.make_async_copy(
                zero_ref, xs_ref.at[pl.ds(pl.multiple_of(m * tile_rows, tile_rows), tile_rows), :], sem)
            cp.start()
            cp.wait()
            return c

        lax.fori_loop(nt_ref[0], n_tiles, tail_tile, 0)


def _dispatch(hp, dest_flat, pad_start, pad_len, n_tiles_used, n_rows):
    t = hp.shape[0] // TOK_SUB
    tm = TOK_TILE
    smem = pl.BlockSpec(memory_space=pltpu.SMEM)
    return pl.pallas_call(
        _dispatch_kernel,
        out_shape=jax.ShapeDtypeStruct((n_rows * TOK_SUB, LANES), u32),
        grid=(t // tm,),
        in_specs=[
            pl.BlockSpec((tm * TOP_K,), lambda i: (i,), memory_space=pltpu.SMEM),
            smem, smem, smem,
            pl.BlockSpec((tm * TOK_SUB, LANES), lambda i: (i, 0)),
        ],
        out_specs=pl.BlockSpec(memory_space=pl.ANY),
        scratch_shapes=[pltpu.VMEM((EXP_TILE * TOK_SUB, LANES), u32), pltpu.SemaphoreType.DMA(())],
        compiler_params=_cparams(("arbitrary",), VMEM_LIMIT_MID),
        name="moe_dispatch",
    )(dest_flat, pad_start, pad_len, n_tiles_used, hp)


def _group_start(te_ref, m):
    return jnp.logical_or(m == 0, te_ref[m] != te_ref[jnp.maximum(m - 1, 0)])


def _gate_up_kernel(te_ref, nx_ref, rv_ref, nt_ref, xs_ref, w_hbm, bg_ref, bu_ref, h_ref,
                    wbuf, sems, grp_ref, *, layer):
    j = pl.program_id(0)
    m = pl.program_id(1)
    n_pass = pl.num_programs(0)
    fc = wbuf.shape[3]
    half = EXP_TILE // 2
    valid = m < nt_ref[0]

    def copies(e, jj, slot):
        col = pl.multiple_of(jj * fc, fc)
        return (pltpu.make_async_copy(w_hbm.at[layer, e, :, pl.ds(col, fc)], wbuf.at[slot, 0], sems.at[slot, 0]),
                pltpu.make_async_copy(w_hbm.at[layer, e, :, pl.ds(pl.multiple_of(D_FF + col, fc), fc)],
                                      wbuf.at[slot, 1], sems.at[slot, 1]))

    @pl.when(jnp.logical_and(_group_start(te_ref, m), valid))
    def _():
        @pl.when(jnp.logical_and(j == 0, m == 0))
        def _():
            grp_ref[0] = 0
            for c in copies(te_ref[0], 0, 0):
                c.start()

        g = grp_ref[0]
        slot = g & 1
        for c in copies(te_ref[m], j, slot):
            c.wait()
        nxt = nx_ref[m]

        @pl.when(nxt >= 0)
        def _():
            for c in copies(nxt, j, 1 - slot):
                c.start()

        @pl.when(jnp.logical_and(nxt < 0, j + 1 < n_pass))
        def _():
            for c in copies(te_ref[0], j + 1, 1 - slot):
                c.start()

        grp_ref[0] = g + 1

    cur = (grp_ref[0] + 1) & 1
    dn = (((1,), (0,)), ((), ()))

    def swiglu(x):
        g_lin = lax.dot_general(x, wbuf[cur, 0], dn, preferred_element_type=f32)
        u_lin = lax.dot_general(x, wbuf[cur, 1], dn, preferred_element_type=f32)
        gate = jnp.minimum(g_lin + bg_ref[...], SWIGLU_LIMIT)
        up = jnp.clip(u_lin + bu_ref[...], -SWIGLU_LIMIT, SWIGLU_LIMIT)
        glu = gate * jax.nn.sigmoid(gate * SWIGLU_ALPHA)
        return _pack_bf16_pairs((up + 1.0) * glu)

    @pl.when(rv_ref[m] > half)
    def _():
        h_ref[...] = swiglu(_load_token_tiles_bf16(xs_ref, EXP_TILE))

    @pl.when(jnp.logical_and(rv_ref[m] > 0, rv_ref[m] <= half))
    def _():
        h_ref[pl.ds(0, half), :] = swiglu(_load_token_tiles_bf16(xs_ref, half))
        h_ref[pl.ds(half, half), :] = jnp.zeros((half, h_ref.shape[1]), h_ref.dtype)

    @pl.when(rv_ref[m] == 0)
    def _():
        h_ref[...] = jnp.zeros_like(h_ref)


def _gate_up(xs, w_gate_up, b_gate_up4, layer, tile_e, next_e, rows_valid, n_tiles_used):
    p = xs.shape[0] // TOK_SUB
    d = D_MODEL
    fc = D_FF // 2
    n_pass = D_FF // fc
    tm = EXP_TILE
    grid_spec = pltpu.PrefetchScalarGridSpec(
        num_scalar_prefetch=4,
        grid=(n_pass, p // tm),
        in_specs=[
            pl.BlockSpec((tm * TOK_SUB, LANES), lambda j, m, te, nx, rv, nt: (m, 0)),
            pl.BlockSpec(memory_space=pl.ANY),
            pl.BlockSpec((None, None, 1, fc), lambda j, m, te, nx, rv, nt: (layer, te[m], 0, j)),
            pl.BlockSpec((None, None, 1, fc), lambda j, m, te, nx, rv, nt: (layer, te[m], 0, n_pass + j)),
        ],
        out_specs=pl.BlockSpec((tm, fc // 2), lambda j, m, te, nx, rv, nt: (m, j)),
        scratch_shapes=[
            pltpu.VMEM((2, 2, d, fc), f32),
            pltpu.SemaphoreType.DMA((2, 2)),
            pltpu.SMEM((1,), i32),
        ],
    )
    return pl.pallas_call(
        functools.partial(_gate_up_kernel, layer=layer),
        out_shape=jax.ShapeDtypeStruct((p, D_FF // 2), u32),
        grid_spec=grid_spec,
        compiler_params=_cparams(("arbitrary", "arbitrary"), VMEM_LIMIT_BIG),
        name="moe_gate_up",
    )(tile_e, next_e, rows_valid, n_tiles_used, xs, w_gate_up, b_gate_up4, b_gate_up4)


def _down_kernel(te_ref, nx_ref, rv_ref, nt_ref, h_ref, w_hbm, b_ref, y_ref, wbuf, sems, grp_ref, *, layer):
    m = pl.program_id(0)
    fc = D_FF // 2
    half = EXP_TILE // 2
    valid = m < nt_ref[0]

    def copy(e, slot):
        return pltpu.make_async_copy(w_hbm.at[layer, e], wbuf.at[slot], sems.at[slot])

    @pl.when(jnp.logical_and(_group_start(te_ref, m), valid))
    def _():
        @pl.when(m == 0)
        def _():
            grp_ref[0] = 0
            copy(te_ref[0], 0).start()

        g = grp_ref[0]
        slot = g & 1
        copy(te_ref[m], slot).wait()
        nxt = nx_ref[m]

        @pl.when(nxt >= 0)
        def _():
            copy(nxt, 1 - slot).start()

        grp_ref[0] = g + 1

    cur = (grp_ref[0] + 1) & 1

    def project(hp):
        h = jnp.concatenate([_unpack_bf16_pairs(hp[:, :fc // 2]), _unpack_bf16_pairs(hp[:, fc // 2:])], axis=1)
        y = lax.dot_general(h, wbuf[cur], (((1,), (0,)), ((), ())), preferred_element_type=f32)
        return _pack_bf16_pairs(y + b_ref[...])

    @pl.when(rv_ref[m] > half)
    def _():
        _store_token_tiles(y_ref, project(h_ref[...]))

    @pl.when(jnp.logical_and(rv_ref[m] > 0, rv_ref[m] <= half))
    def _():
        _store_token_tiles(y_ref, project(h_ref[pl.ds(0, half), :]))
        y_ref[pl.ds(half * TOK_SUB, half * TOK_SUB), :] = jnp.zeros((half * TOK_SUB, LANES), y_ref.dtype)

    @pl.when(rv_ref[m] == 0)
    def _():
        y_ref[...] = jnp.zeros_like(y_ref)


def _down(hs, w_down, b_down4, layer, tile_e, next_e, rows_valid, n_tiles_used):
    p, half = hs.shape
    d_ff = 2 * half
    d = w_down.shape[3]
    tm = EXP_TILE
    grid_spec = pltpu.PrefetchScalarGridSpec(
        num_scalar_prefetch=4,
        grid=(p // tm,),
        in_specs=[
            pl.BlockSpec((tm, half), lambda m, te, nx, rv, nt: (m, 0)),
            pl.BlockSpec(memory_space=pl.ANY),
            pl.BlockSpec((None, None, 1, d), lambda m, te, nx, rv, nt: (layer, te[m], 0, 0)),
        ],
        out_specs=pl.BlockSpec((tm * TOK_SUB, LANES), lambda m, te, nx, rv, nt: (m, 0)),
        scratch_shapes=[
            pltpu.VMEM((2, d_ff, d), f32),
            pltpu.SemaphoreType.DMA((2,)),
            pltpu.SMEM((1,), i32),
        ],
    )
    return pl.pallas_call(
        functools.partial(_down_kernel, layer=layer),
        out_shape=jax.ShapeDtypeStruct((p * TOK_SUB, LANES), u32),
        grid_spec=grid_spec,
        compiler_params=_cparams(("arbitrary",), VMEM_LIMIT_BIG),
        name="moe_down",
    )(tile_e, next_e, rows_valid, n_tiles_used, hs, w_down, b_down4)


def _combine_kernel(dest_ref, dest_next_ref, ys_ref, tw_ref, x1_ref, gf_ref, lng_ref, lnb_ref, o_ref,
                    buf_ref, sems):
    tm = x1_ref.shape[0]
    i = pl.program_id(0)
    n = pl.num_programs(0)
    slot = i & 1

    def token_tile(ref, r):
        return ref.at[pl.ds(pl.multiple_of(r * TOK_SUB, TOK_SUB), TOK_SUB), :]

    def copy(dref, s, tok, k):
        return pltpu.make_async_copy(token_tile(ys_ref, dref[tok * TOP_K + k]),
                                     token_tile(buf_ref, (s * TOP_K + k) * tm + tok), sems.at[s])

    def start_all(dref, s):
        def body(tok, c):
            for k in range(TOP_K):
                copy(dref, s, tok, k).start(priority=k % 2)
            return c
        lax.fori_loop(0, tm, body, 0, unroll=2)

    @pl.when(i == 0)
    def _():
        start_all(dest_ref, 0)

    @pl.when(i + 1 < n)
    def _():
        start_all(dest_next_ref, 1 - slot)

    def wait_body(tok, c):
        for k in range(TOP_K):
            copy(dest_ref, slot, tok, k).wait()
        return c

    lax.fori_loop(0, tm, wait_body, 0, unroll=2)

    tw = tw_ref[...]
    y = jnp.zeros((tm, D_MODEL), f32)
    for k in range(TOP_K):
        y = y + tw[:, k:k + 1] * _load_token_tiles_bf16(buf_ref, tm, row0=(slot * TOP_K + k) * tm).astype(f32)
    o_ref[...] = _layer_norm(DEEPNORM_ALPHA * x1_ref[...] + gf_ref[...] * y, lng_ref[...], lnb_ref[...])


def _combine(ys, dest_flat, top_w, x1, mod3, ln_g, ln_b, seq):
    t, d = x1.shape
    tm = TOK_TILE
    tps = seq // tm
    n_steps = t // tm
    grid_spec = pltpu.PrefetchScalarGridSpec(
        num_scalar_prefetch=0,
        grid=(n_steps,),
        in_specs=[
            pl.BlockSpec((tm * TOP_K,), lambda i: (i,), memory_space=pltpu.SMEM),
            pl.BlockSpec((tm * TOP_K,), lambda i: (jnp.minimum(i + 1, n_steps - 1),), memory_space=pltpu.SMEM),
            pl.BlockSpec(memory_space=pl.ANY),
            pl.BlockSpec((tm, TOP_K), lambda i: (i, 0)),
            pl.BlockSpec((tm, d), lambda i: (i, 0)),
            pl.BlockSpec((None, 1, d), lambda i: (i // tps, 0, 5)),
            pl.BlockSpec((1, d), lambda i: (0, 0)),
            pl.BlockSpec((1, d), lambda i: (0, 0)),
        ],
        out_specs=pl.BlockSpec((tm, d), lambda i: (i, 0)),
        scratch_shapes=[pltpu.VMEM((2 * TOP_K * tm * TOK_SUB, LANES), u32), pltpu.SemaphoreType.DMA((2,))],
    )
    return pl.pallas_call(
        _combine_kernel,
        out_shape=jax.ShapeDtypeStruct((t, d), f32),
        grid_spec=grid_spec,
        compiler_params=_cparams(("arbitrary",), VMEM_LIMIT_MID),
        name="moe_combine",
    )(dest_flat, dest_flat, ys, top_w, x1, mod3, ln_g.reshape(1, -1), ln_b.reshape(1, -1))


def _routing_tables(top_e, rank, counts, n_tiles):
    tm = EXP_TILE
    counts = counts.reshape(-1)
    padded = (counts + tm - 1) // tm * tm
    pend = jnp.cumsum(padded)
    pstart = pend - padded
    dest = (pstart[top_e] + rank).reshape(-1).astype(i32)
    tile_start = jnp.arange(n_tiles, dtype=i32) * tm
    tile_e = jnp.sum((pend[None, :] <= tile_start[:, None]).astype(i32), axis=1)
    tile_e = jnp.minimum(tile_e, N_EXPERTS - 1).astype(i32)
    n_used = (pend[-1] // tm).astype(i32).reshape(1)
    mine = jnp.arange(N_EXPERTS, dtype=i32)[None, :] == tile_e[:, None]
    group_end = jnp.sum(jnp.where(mine, pend[None, :], 0), axis=1)
    data_end = jnp.sum(jnp.where(mine, (pstart + counts)[None, :], 0), axis=1)
    rows_valid = jnp.where(tile_start < pend[-1], jnp.clip(data_end - tile_start, 0, tm), 0).astype(i32)
    next_e = jnp.sum((pend[None, :] <= group_end[:, None]).astype(i32), axis=1)
    next_e = jnp.where(group_end < pend[-1], jnp.minimum(next_e, N_EXPERTS - 1), -1).astype(i32)
    return dest, tile_e, next_e, rows_valid, n_used, (pstart + counts).astype(i32), (padded - counts).astype(i32)


def kernel(x, c, rel_bias, w_ada, b_ada, w_in, conv_w, conv_b, lru_w_a, lru_b_a, lru_w_x, lru_b_x, lru_lambda, norm_lru_g, norm_attn_g, w_out, ln_mix_g, ln_mix_b, w_router, b_router, w_gate_up, b_gate_up, w_down, b_down, ln_ffn_g, ln_ffn_b):
    nb, seq, d = x.shape
    t = nb * seq
    mod = _ada_mod(c, w_ada, b_ada)
    b1, b2, b3 = _attn_bias_tables(rel_bias)
    n_tiles = t * TOP_K // EXP_TILE + N_EXPERTS
    q_lo = 2 * D_LRU
    x2 = x.reshape(t, d)
    for l in range(DEPTH):
        mod3 = mod[l].reshape(nb, 1, 6 * d)
        w_in_l = w_in[l].at[:, q_lo:q_lo + D_ATTN].multiply(HEAD_DIM ** -0.5).astype(bf16)
        proj = _in_proj(x2, mod3, w_in_l, seq).reshape(nb, seq, D_IN)
        wg = jnp.concatenate([lru_w_a[l], lru_w_x[l]], axis=-1).astype(bf16)
        bg = jnp.concatenate([lru_b_a[l].reshape(2, N_LRU_BLOCKS, 1, LRU_BLOCK),
                              lru_b_x[l].reshape(2, N_LRU_BLOCKS, 1, LRU_BLOCK)], axis=-1)
        lru = _lru_group(proj, conv_w[l], conv_b[l], wg, bg, lru_lambda[l])
        att = _dilated_attention(proj, b1, b2, b3)
        x1, hp, top_e, top_w, rank, counts = _out_proj_router(
            lru.reshape(t, D_LRU), att.reshape(t, D_ATTN), x2, norm_lru_g[l], norm_attn_g[l],
            w_out[l].astype(bf16), mod3, ln_mix_g[l], ln_mix_b[l], w_router[l], b_router[l], seq)
        dest, tile_e, next_e, rows_valid, n_used, pad_start, pad_len = _routing_tables(top_e, rank, counts, n_tiles)
        xs = _dispatch(hp, dest, pad_start, pad_len, n_used, n_tiles * EXP_TILE)
        hs = _gate_up(xs, w_gate_up, b_gate_up.reshape(DEPTH, N_EXPERTS, 1, 2 * D_FF), l,
                      tile_e, next_e, rows_valid, n_used)
        ys = _down(hs, w_down, b_down.reshape(DEPTH, N_EXPERTS, 1, d), l, tile_e, next_e, rows_valid, n_used)
        x2 = _combine(ys, dest, top_w, x1, mod3, ln_ffn_g[l], ln_ffn_b[l], seq)
    return x2.reshape(nb, seq, d)
```

```python
import functools
import math

import jax
import jax.numpy as jnp
from jax import lax
from jax.experimental import pallas as pl
from jax.experimental.pallas import tpu as pltpu

f32 = jnp.float32
bf16 = jnp.bfloat16
i32 = jnp.int32
u32 = jnp.uint32

D_MODEL = 2048
DEPTH = 2
D_LRU = 1024
N_LRU_BLOCKS = 8
LRU_BLOCK = 128
CONV_WIDTH = 4
LRU_C = 8.0
D_ATTN = 1024
HEAD_DIM = 128
N_HEADS = 8
D_IN = 2 * D_LRU + 3 * D_ATTN
NEG_INF = -1e30
N_BUCKETS = 32
MAX_EXACT = 8
MAX_DISTANCE = 1024
N_EXPERTS = 32
TOP_K = 4
D_FF = D_MODEL
SWIGLU_LIMIT = 7.0
SWIGLU_ALPHA = 1.702
DEEPNORM_ALPHA = (2 * DEPTH) ** 0.25
LN_EPS = 1e-5

VMEM_LIMIT_BIG = 56 * 1024 * 1024
VMEM_LIMIT_MID = 40 * 1024 * 1024

ROW_TILE = 512
ROUTER_TILE = 512
TOK_TILE = 256
EXP_TILE = 512
LRU_CHUNK = 128
Q_BLOCK = 128
NAT_BLOCK = 256
SUB_ROWS = 128
HALF = 64
RES16 = 16
STAGE_PITCH = 24


def _cparams(sem, vmem=None):
    return pltpu.CompilerParams(dimension_semantics=sem, vmem_limit_bytes=vmem)


def _ada_kernel(c_ref, w_ref, b_ref, o_ref):
    c = c_ref[...]
    ca = c * jax.nn.sigmoid(c)
    o_ref[0] = jnp.dot(ca, w_ref[0], precision=lax.Precision.HIGHEST,
                       preferred_element_type=f32) + b_ref[0]


def _ada_mod(c, w_ada, b_ada):
    depth, d, n = w_ada.shape
    b = c.shape[0]
    tn = 1024
    return pl.pallas_call(
        _ada_kernel,
        out_shape=jax.ShapeDtypeStruct((depth, b, n), f32),
        grid=(depth, n // tn),
        in_specs=[
            pl.BlockSpec((b, d), lambda l, j: (0, 0)),
            pl.BlockSpec((1, d, tn), lambda l, j: (l, 0, j)),
            pl.BlockSpec((1, 1, tn), lambda l, j: (l, 0, j)),
        ],
        out_specs=pl.BlockSpec((1, b, tn), lambda l, j: (l, 0, j)),
        compiler_params=_cparams(("parallel", "parallel"), VMEM_LIMIT_MID),
        name="ada_mod",
    )(c, w_ada, b_ada.reshape(depth, 1, n))


def _inproj_kernel(x_ref, sh_ref, sc_ref, w_ref, o_ref):
    h = x_ref[...] * (1.0 + sc_ref[...]) + sh_ref[...]
    o_ref[...] = jnp.dot(h.astype(bf16), w_ref[...], preferred_element_type=f32).astype(o_ref.dtype)


def _in_proj(x2, mod3, w_in_bf, seq):
    t, d = x2.shape
    n = w_in_bf.shape[1]
    tn = n // 2
    tiles_per_seq = seq // ROW_TILE
    return pl.pallas_call(
        _inproj_kernel,
        out_shape=jax.ShapeDtypeStruct((t, n), bf16),
        grid=(n // tn, t // ROW_TILE),
        in_specs=[
            pl.BlockSpec((ROW_TILE, d), lambda j, i: (i, 0)),
            pl.BlockSpec((None, 1, d), lambda j, i: (i // tiles_per_seq, 0, 0)),
            pl.BlockSpec((None, 1, d), lambda j, i: (i // tiles_per_seq, 0, 1)),
            pl.BlockSpec((d, tn), lambda j, i: (0, j)),
        ],
        out_specs=pl.BlockSpec((ROW_TILE, tn), lambda j, i: (i, j)),
        compiler_params=_cparams(("parallel", "parallel"), VMEM_LIMIT_BIG),
        name="in_proj",
    )(x2, mod3, mod3, w_in_bf)


def _lru_kernel(xb_ref, gb_ref, cw_ref, cb_ref, wg_ref, bg_ref, lam_ref, o_ref,
                xt_ref, hs_ref, a_ref, b_ref, hb_ref):
    nb, seq, _ = xb_ref.shape
    tc = LRU_CHUNK
    rows = tc * nb
    n_chunks = seq // tc
    front = (CONV_WIDTH // 2) * nb

    xt_ref[pl.ds(0, front), :] = jnp.zeros((front, LRU_BLOCK), f32)
    xt_ref[pl.ds(front + seq * nb, nb), :] = jnp.zeros((nb, LRU_BLOCK), f32)
    for b in range(nb):
        xt_ref[pl.ds(front + b, seq, stride=nb), :] = xb_ref[b].astype(f32)

    lam = lam_ref[...]
    neg_c_sp = -LRU_C * jax.nn.softplus(-lam)
    cw = cw_ref[...]
    cb = cb_ref[...]

    def gates(d, r0):
        xc = cb
        for j in range(CONV_WIDTH):
            xc = xc + cw[j:j + 1, :] * xt_ref[pl.ds(pl.multiple_of(r0 + j * nb, nb), rows), :]
        g = jnp.dot(xc.astype(bf16), wg_ref[d], preferred_element_type=f32) + bg_ref[d]
        r = 0.5 + 0.5 * jnp.tanh(0.5 * g[:, :LRU_BLOCK])
        i = 0.5 + 0.5 * jnp.tanh(0.5 * g[:, LRU_BLOCK:])
        log_a = neg_c_sp[d:d + 1, :] * r
        a = jnp.exp(log_a)
        u = a * a
        z = 2.0 * log_a
        one_minus = jnp.where(u == 1.0, -z, (1.0 - u) * z / jnp.log(u))
        root = jnp.where(one_minus > 0.0, one_minus * lax.rsqrt(one_minus), 0.0)
        a_ref[...] = a
        b_ref[...] = root * (i * xc)

    def fwd_chunk(c, h):
        r0 = pl.multiple_of(c * rows, rows)
        gates(0, r0)

        def step(t, h):
            rr = pl.multiple_of(t * nb, nb)
            h = a_ref[pl.ds(rr, nb), :] * h + b_ref[pl.ds(rr, nb), :]
            hs_ref[pl.ds(r0 + rr, nb), :] = h
            return h

        return lax.fori_loop(0, tc, step, h, unroll=8)

    lax.fori_loop(0, n_chunks, fwd_chunk, jnp.zeros((nb, LRU_BLOCK), f32))

    def bwd_chunk(ci, h):
        c = n_chunks - 1 - ci
        r0 = pl.multiple_of(c * rows, rows)
        gates(1, r0)

        def step(ti, h):
            rr = pl.multiple_of((tc - 1 - ti) * nb, nb)
            h = a_ref[pl.ds(rr, nb), :] * h + b_ref[pl.ds(rr, nb), :]
            hb_ref[pl.ds(rr, nb), :] = h
            return h

        h = lax.fori_loop(0, tc, step, h, unroll=8)
        hs_ref[pl.ds(r0, rows), :] = hs_ref[pl.ds(r0, rows), :] + hb_ref[...]
        t0 = pl.multiple_of(c * tc, tc)
        for b in range(nb):
            hr = hs_ref[pl.ds(r0 + b, tc, stride=nb), :]
            gate = jax.nn.gelu(gb_ref[b, pl.ds(t0, tc), :].astype(f32))
            o_ref[b, pl.ds(t0, tc), :] = (hr * gate).astype(o_ref.dtype)
        return h

    lax.fori_loop(0, n_chunks, bwd_chunk, jnp.zeros((nb, LRU_BLOCK), f32))


def _lru_group(proj3, conv_w, conv_b, wg, bg, lam):
    nb, seq, _ = proj3.shape
    g = N_LRU_BLOCKS
    return pl.pallas_call(
        _lru_kernel,
        out_shape=jax.ShapeDtypeStruct((nb, seq, D_LRU), bf16),
        grid=(g,),
        in_specs=[
            pl.BlockSpec((nb, seq, LRU_BLOCK), lambda j: (0, 0, j)),
            pl.BlockSpec((nb, seq, LRU_BLOCK), lambda j: (0, 0, g + j)),
            pl.BlockSpec((CONV_WIDTH, LRU_BLOCK), lambda j: (0, j)),
            pl.BlockSpec((1, LRU_BLOCK), lambda j: (0, j)),
            pl.BlockSpec((2, None, LRU_BLOCK, 2 * LRU_BLOCK), lambda j: (0, j, 0, 0)),
            pl.BlockSpec((2, None, 1, 2 * LRU_BLOCK), lambda j: (0, j, 0, 0)),
            pl.BlockSpec((2, LRU_BLOCK), lambda j: (0, j)),
        ],
        out_specs=pl.BlockSpec((nb, seq, LRU_BLOCK), lambda j: (0, 0, j)),
        scratch_shapes=[
            pltpu.VMEM(((seq + CONV_WIDTH - 1) * nb, LRU_BLOCK), f32),
            pltpu.VMEM((seq * nb, LRU_BLOCK), f32),
            pltpu.VMEM((LRU_CHUNK * nb, LRU_BLOCK), f32),
            pltpu.VMEM((LRU_CHUNK * nb, LRU_BLOCK), f32),
            pltpu.VMEM((LRU_CHUNK * nb, LRU_BLOCK), f32),
        ],
        compiler_params=_cparams(("parallel",), VMEM_LIMIT_BIG),
        name="rg_lru",
    )(proj3, proj3, conv_w, conv_b.reshape(1, D_LRU), wg, bg, lam)


def _t5_bucket(rel):
    nbk = N_BUCKETS // 2
    ret = jnp.where(rel > 0, nbk, 0)
    n = jnp.abs(rel)
    nf = jnp.maximum(n, 1).astype(f32)
    large = MAX_EXACT + (jnp.log(nf / MAX_EXACT) / math.log(MAX_DISTANCE / MAX_EXACT)
                         * (nbk - MAX_EXACT)).astype(i32)
    large = jnp.minimum(large, nbk - 1)
    return ret + jnp.where(n < MAX_EXACT, n, large)


def _attn_bias_tables(rel_bias):
    qb = Q_BLOCK
    rb = rel_bias.astype(f32)
    i = jnp.arange(qb, dtype=i32)[:, None]

    def tile(step, dilation, valid):
        onehot = jax.nn.one_hot(_t5_bucket(step * dilation), N_BUCKETS, dtype=f32)
        t = jnp.einsum("qkn,nh->hqk", onehot, rb, precision=lax.Precision.HIGHEST)
        return jnp.where(valid[None], t, NEG_INF)

    i1 = jnp.arange(NAT_BLOCK, dtype=i32)[:, None]
    win = NAT_BLOCK + 2 * HALF
    j = jnp.arange(win, dtype=i32)[None, :]
    step1 = j - HALF - i1
    band1 = jnp.abs(step1) <= HALF
    b1 = jnp.stack([tile(step1, 1, band1),
                    tile(step1, 1, band1 & (j >= HALF)),
                    tile(step1, 1, band1 & (j < win - HALF))],
                   axis=1)
    lk = jnp.arange(qb, dtype=i32)[None, :]
    step2 = jnp.concatenate(
        [jnp.concatenate([4 * (lk - i) + (jk - jq) for jk in range(4)], axis=1) for jq in range(4)], axis=0)
    b2 = tile(step2, 4, jnp.abs(step2) <= HALF)
    step3 = lk - i
    b3 = tile(step3, 16, jnp.abs(step3) <= HALF)
    return b1, b2, b3


def _lane_tiles(a):
    return [a[:, c:c + LANES] for c in range(0, a.shape[1], LANES)] if a.shape[1] % LANES == 0 else [a]


def _softmax_numer(s):
    m = jnp.max(functools.reduce(jnp.maximum, _lane_tiles(s)), axis=-1, keepdims=True)
    return m, jnp.exp(s - m).astype(bf16)


def _pv_and_rowsum(p, v_aug):
    o = jnp.dot(p, v_aug, preferred_element_type=f32)
    return o[:, :HEAD_DIM], o[:, HEAD_DIM:]


def _attn_kernel(q_ref, k_ref, v_ref, b1_ref, b2_ref, b3_ref, o_ref,
                 stage_ref, qp_ref, kp_ref, vp_ref, kpad_ref, vpad_ref, o23_ref, l23_ref):
    seq = q_ref.shape[0]
    qb = NAT_BLOCK
    nblk = seq // qb
    per_res = seq // RES16
    cls = 4 * per_res
    dn = (((1,), (1,)), ((), ()))

    def perm_row0(r16):
        return ((r16 % 4) * 4 + r16 // 4) * per_res

    def pitched(ref, g, rows=RES16):
        return ref.at[pl.ds(pl.multiple_of(g * STAGE_PITCH, 8), rows), :]

    for src, dst in ((q_ref, qp_ref), (k_ref, kp_ref), (v_ref, vp_ref)):
        def fill(g, c, src=src):
            pitched(stage_ref, g)[...] = src[pl.ds(pl.multiple_of(g * RES16, RES16), RES16), :].astype(f32)
            return c

        lax.fori_loop(0, per_res, fill, 0, unroll=8)
        for r in range(RES16):
            dst[pl.ds(perm_row0(r), per_res), pl.ds(0, HEAD_DIM)] = (
                stage_ref[pl.ds(r, per_res, stride=STAGE_PITCH), :].astype(bf16))

    head = pl.ds(0, HEAD_DIM)
    ones_cols = pl.ds(HEAD_DIM, HEAD_DIM)
    zpad = jnp.zeros((HALF, HEAD_DIM), bf16)
    kpad_ref[pl.ds(0, HALF), :] = zpad
    vpad_ref[pl.ds(0, HALF), head] = zpad
    kpad_ref[pl.ds(HALF + seq, HALF), :] = zpad
    vpad_ref[pl.ds(HALF + seq, HALF), head] = zpad
    kpad_ref[pl.ds(HALF, seq), :] = k_ref[...]
    vpad_ref[pl.ds(HALF, seq), head] = v_ref[...]
    vpad_ref[:, ones_cols] = jnp.ones((seq + 2 * HALF, HEAD_DIM), bf16)
    vp_ref[:, ones_cols] = jnp.ones((seq, HEAD_DIM), bf16)

    for r4 in range(4):
        base = r4 * cls
        q4 = qp_ref[pl.ds(base, cls), :]
        k4 = kp_ref[pl.ds(base, cls), :]
        v4 = vp_ref[pl.ds(base, cls), :]
        raw = lax.dot_general(q4, k4, dn, preferred_element_type=f32)
        m2, p2 = _softmax_numer(raw + b2_ref[...])
        o2, d2 = _pv_and_rowsum(p2, v4)
        m3s, d3s, o3s = [], [], []
        for jq in range(4):
            lo, hi = jq * per_res, (jq + 1) * per_res
            m3, p3 = _softmax_numer(raw[lo:hi, lo:hi] + b3_ref[...])
            o3, d3 = _pv_and_rowsum(p3, v4[lo:hi])
            m3s.append(m3)
            d3s.append(d3)
            o3s.append(o3)
        m3 = jnp.concatenate(m3s, axis=0)
        d3 = jnp.concatenate(d3s, axis=0)
        o3 = jnp.concatenate(o3s, axis=0)
        m23 = jnp.maximum(m2, m3)
        w2 = jnp.exp(m2 - m23)
        w3 = jnp.exp(m3 - m23)
        d23 = w2 * d2 + w3 * d3
        o23_ref[pl.ds(base, cls), :] = (w2 * o2 + w3 * o3) / d23
        l23_ref[pl.ds(base, cls), :] = m23 + jnp.log(d23)

    for r in range(RES16):
        stage_ref[pl.ds(r, per_res, stride=STAGE_PITCH), :] = o23_ref[pl.ds(perm_row0(r), per_res), :]
    for r in range(RES16):
        o23_ref[pl.ds(r, per_res, stride=STAGE_PITCH), :] = l23_ref[pl.ds(perm_row0(r), per_res), :]

    def natural_rows(ref, g0, rows):
        return jnp.concatenate([pitched(ref, g0 + i)[...] for i in range(rows // RES16)], axis=0)

    per_group = 4
    n_groups = nblk // per_group

    def nat_group(gi, carry):
        for sb in range(per_group):
            row0 = pl.multiple_of((gi * per_group + sb) * qb, qb)
            q = q_ref[pl.ds(row0, qb), :]
            kw = kpad_ref[pl.ds(row0, qb + 2 * HALF), :]
            vw = vpad_ref[pl.ds(row0, qb + 2 * HALF), :]
            if sb == 0:
                bias = b1_ref[jnp.where(gi == 0, 1, 0)]
            elif sb == per_group - 1:
                bias = b1_ref[jnp.where(gi == n_groups - 1, 2, 0)]
            else:
                bias = b1_ref[0]
            s1 = lax.dot_general(q, kw, dn, preferred_element_type=f32) + bias
            m1, p1 = _softmax_numer(s1)
            o1, d1 = _pv_and_rowsum(p1, vw)
            g0 = (gi * per_group + sb) * (qb // RES16)
            l23 = natural_rows(o23_ref, g0, qb)
            on23 = natural_rows(stage_ref, g0, qb)
            mm = jnp.maximum(m1, l23)
            w1 = jnp.exp(m1 - mm)
            w23 = jnp.exp(l23 - mm)
            o_ref[pl.ds(row0, qb), :] = ((w1 * o1 + w23 * on23) / (w1 * d1 + w23)).astype(o_ref.dtype)
        return carry

    lax.fori_loop(0, n_groups, nat_group, 0)


def _dilated_attention(proj3, b1, b2, b3):
    nb, seq, _ = proj3.shape
    assert seq // RES16 == Q_BLOCK and seq % (4 * NAT_BLOCK) == 0
    q0 = 2 * D_LRU // HEAD_DIM
    k0 = q0 + N_HEADS
    v0 = k0 + N_HEADS
    return pl.pallas_call(
        _attn_kernel,
        out_shape=jax.ShapeDtypeStruct((nb, seq, D_ATTN), bf16),
        grid=(N_HEADS, nb),
        in_specs=[
            pl.BlockSpec((None, seq, HEAD_DIM), lambda h, b: (b, 0, q0 + h)),
            pl.BlockSpec((None, seq, HEAD_DIM), lambda h, b: (b, 0, k0 + h)),
            pl.BlockSpec((None, seq, HEAD_DIM), lambda h, b: (b, 0, v0 + h)),
            pl.BlockSpec((None, 3, NAT_BLOCK, NAT_BLOCK + 2 * HALF), lambda h, b: (h, 0, 0, 0)),
            pl.BlockSpec((None, 4 * Q_BLOCK, 4 * Q_BLOCK), lambda h, b: (h, 0, 0)),
            pl.BlockSpec((None, Q_BLOCK, Q_BLOCK), lambda h, b: (h, 0, 0)),
        ],
        out_specs=pl.BlockSpec((None, seq, HEAD_DIM), lambda h, b: (b, 0, h)),
        scratch_shapes=[
            pltpu.VMEM((seq // RES16 * STAGE_PITCH, HEAD_DIM), f32),
            pltpu.VMEM((seq, HEAD_DIM), bf16),
            pltpu.VMEM((seq, HEAD_DIM), bf16),
            pltpu.VMEM((seq, 2 * HEAD_DIM), bf16),
            pltpu.VMEM((seq + 2 * HALF, HEAD_DIM), bf16),
            pltpu.VMEM((seq + 2 * HALF, 2 * HEAD_DIM), bf16),
            pltpu.VMEM((seq // RES16 * STAGE_PITCH, HEAD_DIM), f32),
            pltpu.VMEM((seq, HEAD_DIM), f32),
        ],
        compiler_params=_cparams(("parallel", "parallel"), VMEM_LIMIT_MID),
        name="dilated_attn",
    )(proj3, proj3, proj3, b1, b2, b3)


def _layer_norm(z, g, b):
    mu = jnp.mean(z, axis=-1, keepdims=True)
    zc = z - mu
    var = jnp.mean(zc * zc, axis=-1, keepdims=True)
    return zc * lax.rsqrt(var + LN_EPS) * g + b


def _rms_norm(v, g):
    return v * lax.rsqrt(jnp.mean(v * v, axis=-1, keepdims=True) + LN_EPS) * g


def _pack_bf16_pairs(v):
    n = v.shape[1] // 2
    lo = pltpu.bitcast(v[:, :n].astype(bf16).astype(f32), u32)
    hi = pltpu.bitcast(v[:, n:].astype(bf16).astype(f32), u32)
    return (lo >> 16) | (hi & jnp.uint32(0xFFFF0000))


def _unpack_bf16_pairs(p):
    lo = pltpu.bitcast(p << 16, f32)
    hi = pltpu.bitcast(p & jnp.uint32(0xFFFF0000), f32)
    return jnp.concatenate([lo.astype(bf16), hi.astype(bf16)], axis=1)


LANES = 128
TOK_SUB = D_MODEL // 2 // LANES


def _store_token_tiles(ref, packed, row0=0):
    rows = packed.shape[0]
    for c in range(TOK_SUB):
        ref[pl.ds(TOK_SUB * row0 + c, rows, stride=TOK_SUB), :] = packed[:, c * LANES:(c + 1) * LANES]


def _load_token_tiles_bf16(ref, rows, row0=0):
    los, his = [], []
    for c in range(TOK_SUB):
        p = ref[pl.ds(TOK_SUB * row0 + c, rows, stride=TOK_SUB), :]
        los.append(pltpu.bitcast(p << 16, f32).astype(bf16))
        his.append(pltpu.bitcast(p & jnp.uint32(0xFFFF0000), f32).astype(bf16))
    return jnp.concatenate(los + his, axis=1)


def _outproj_kernel(lru_ref, att_ref, x_ref, gl_ref, ga_ref, w_ref, gm_ref, shf_ref, scf_ref,
                    lng_ref, lnb_ref, wrc_ref, br_ref,
                    x1_ref, hp_ref, te_ref, tw_ref, rk_ref, cnt_ref, carry_ref):
    tm = x_ref.shape[0]
    step = pl.program_id(0)

    @pl.when(step == 0)
    def _():
        carry_ref[...] = jnp.zeros_like(carry_ref)

    ln = _rms_norm(lru_ref[...].astype(f32), gl_ref[...]).astype(bf16)
    an = _rms_norm(att_ref[...].astype(f32), ga_ref[...]).astype(bf16)
    y = (jnp.dot(ln, w_ref[pl.ds(0, D_LRU), :], preferred_element_type=f32)
         + jnp.dot(an, w_ref[pl.ds(D_LRU, D_ATTN), :], preferred_element_type=f32))
    x1 = _layer_norm(DEEPNORM_ALPHA * x_ref[...] + gm_ref[...] * y, lng_ref[...], lnb_ref[...])
    x1_ref[...] = x1
    hf = x1 * (1.0 + scf_ref[...]) + shf_ref[...]
    _store_token_tiles(hp_ref, _pack_bf16_pairs(hf))

    hf_hi = hf.astype(bf16)
    hf_lo = (hf - hf_hi.astype(f32)).astype(bf16)
    hi_both = jnp.dot(hf_hi, wrc_ref[...], preferred_element_type=f32)
    lo_hi = jnp.dot(hf_lo, wrc_ref[:, :N_EXPERTS], preferred_element_type=f32)
    logits = hi_both[:, :N_EXPERTS] + (lo_hi + hi_both[:, N_EXPERTS:]) + br_ref[...]
    lane = lax.broadcasted_iota(i32, (tm, N_EXPERTS), 1)
    col4 = lax.broadcasted_iota(i32, (tm, TOP_K), 1)
    work = logits
    vals, idxs = [], []
    for _k in range(TOP_K):
        m = jnp.max(work, axis=-1, keepdims=True)
        idx = jnp.min(jnp.where(work == m, lane, N_EXPERTS), axis=-1, keepdims=True)
        vals.append(m)
        idxs.append(idx)
        work = jnp.where(lane == idx, -jnp.inf, work)
    exps = [jnp.exp(v - vals[0]) for v in vals]
    den = exps[0] + exps[1] + exps[2] + exps[3]

    onehot = jnp.zeros((tm, N_EXPERTS), f32)
    for idx in idxs:
        onehot = onehot + (lane == idx).astype(f32)
    ri = lax.broadcasted_iota(i32, (tm, tm), 0)
    ci = lax.broadcasted_iota(i32, (tm, tm), 1)
    lower = (ri > ci).astype(bf16)
    before = jnp.dot(lower, onehot.astype(bf16), preferred_element_type=f32) + carry_ref[...]
    te = jnp.zeros((tm, TOP_K), i32)
    tw = jnp.zeros((tm, TOP_K), f32)
    rk = jnp.zeros((tm, TOP_K), i32)
    for k in range(TOP_K):
        rank_k = jnp.sum(jnp.where(lane == idxs[k], before, 0.0), axis=-1, keepdims=True)
        te = jnp.where(col4 == k, idxs[k], te)
        tw = jnp.where(col4 == k, exps[k] / den, tw)
        rk = jnp.where(col4 == k, rank_k.astype(i32), rk)
    te_ref[...] = te
    tw_ref[...] = tw
    rk_ref[...] = rk
    carry_ref[...] = carry_ref[...] + jnp.sum(onehot, axis=0, keepdims=True)
    cnt_ref[...] = carry_ref[...].astype(i32)


def _out_proj_router(lru2, att2, x2, g_lru, g_attn, w_out_bf, mod3, ln_g, ln_b, w_router, b_router, seq):
    t, d = x2.shape
    tm = ROUTER_TILE
    tps = seq // tm
    row = lambda i: (i, 0)
    const = lambda i: (0, 0)
    wr_hi = w_router.astype(bf16)
    wr_cat = jnp.concatenate([wr_hi, (w_router - wr_hi.astype(f32)).astype(bf16)], axis=1)
    return pl.pallas_call(
        _outproj_kernel,
        out_shape=(
            jax.ShapeDtypeStruct((t, d), f32),
            jax.ShapeDtypeStruct((t * TOK_SUB, LANES), u32),
            jax.ShapeDtypeStruct((t, TOP_K), i32),
            jax.ShapeDtypeStruct((t, TOP_K), f32),
            jax.ShapeDtypeStruct((t, TOP_K), i32),
            jax.ShapeDtypeStruct((1, N_EXPERTS), i32),
        ),
        grid=(t // tm,),
        in_specs=[
            pl.BlockSpec((tm, D_LRU), row),
            pl.BlockSpec((tm, D_ATTN), row),
            pl.BlockSpec((tm, d), row),
            pl.BlockSpec((1, D_LRU), const),
            pl.BlockSpec((1, D_ATTN), const),
            pl.BlockSpec((d, d), const),
            pl.BlockSpec((None, 1, d), lambda i: (i // tps, 0, 2)),
            pl.BlockSpec((None, 1, d), lambda i: (i // tps, 0, 3)),
            pl.BlockSpec((None, 1, d), lambda i: (i // tps, 0, 4)),
            pl.BlockSpec((1, d), const),
            pl.BlockSpec((1, d), const),
            pl.BlockSpec((d, 2 * N_EXPERTS), const),
            pl.BlockSpec((1, N_EXPERTS), const),
        ],
        out_specs=(
            pl.BlockSpec((tm, d), row),
            pl.BlockSpec((tm * TOK_SUB, LANES), row),
            pl.BlockSpec((tm, TOP_K), row),
            pl.BlockSpec((tm, TOP_K), row),
            pl.BlockSpec((tm, TOP_K), row),
            pl.BlockSpec((1, N_EXPERTS), const),
        ),
        scratch_shapes=[pltpu.VMEM((1, N_EXPERTS), f32)],
        compiler_params=_cparams(("arbitrary",), VMEM_LIMIT_BIG),
        name="out_proj_router",
    )(lru2, att2, x2, g_lru.reshape(1, -1), g_attn.reshape(1, -1), w_out_bf, mod3, mod3, mod3,
      ln_g.reshape(1, -1), ln_b.reshape(1, -1), wr_cat, b_router.reshape(1, -1))


def _dispatch_kernel(dest_ref, pad_start_ref, pad_len_ref, nt_ref, hp_ref, xs_ref, zero_ref, sem):
    tm = hp_ref.shape[0] // TOK_SUB
    tile_rows = EXP_TILE * TOK_SUB
    n_tiles = xs_ref.shape[0] // tile_rows

    def token_tile(ref, r):
        return ref.at[pl.ds(pl.multiple_of(r * TOK_SUB, TOK_SUB), TOK_SUB), :]

    def copy(tok, k):
        return pltpu.make_async_copy(token_tile(hp_ref, tok), token_tile(xs_ref, dest_ref[tok * TOP_K + k]), sem)

    def start(tok, c):
        for k in range(TOP_K):
            copy(tok, k).start(priority=k % 2)
        return c

    def wait(tok, c):
        for k in range(TOP_K):
            copy(tok, k).wait()
        return c

    lax.fori_loop(0, tm, start, 0, unroll=2)
    lax.fori_loop(0, tm, wait, 0, unroll=2)

    @pl.when(pl.program_id(0) == pl.num_programs(0) - 1)
    def _():
        zero_ref[...] = jnp.zeros_like(zero_ref)

        def per_expert(e, c):
            p0 = pad_start_ref[e]
            n = pad_len_ref[e]

            def zcopy(r):
                return pltpu.make_async_copy(token_tile(zero_ref, 0), token_tile(xs_ref, p0 + r), sem)

            lax.fori_loop(0, n, lambda r, c2: (zcopy(r).start(), c2)[1], 0)
            lax.fori_loop(0, n, lambda r, c2: (zcopy(r).wait(), c2)[1], 0)
            return c

        lax.fori_loop(0, N_EXPERTS, per_expert, 0)

        def tail_tile(m, c):
            cp = pltpu.make_async_copy(
                zero_ref, xs_ref.at[pl.ds(pl.multiple_of(m * tile_rows, tile_rows), tile_rows), :], sem)
            cp.start()
            cp.wait()
            return c

        lax.fori_loop(nt_ref[0], n_tiles, tail_tile, 0)


def _dispatch(hp, dest_flat, pad_start, pad_len, n_tiles_used, n_rows):
    t = hp.shape[0] // TOK_SUB
    tm = TOK_TILE
    smem = pl.BlockSpec(memory_space=pltpu.SMEM)
    return pl.pallas_call(
        _dispatch_kernel,
        out_shape=jax.ShapeDtypeStruct((n_rows * TOK_SUB, LANES), u32),
        grid=(t // tm,),
        in_specs=[
            pl.BlockSpec((tm * TOP_K,), lambda i: (i,), memory_space=pltpu.SMEM),
            smem, smem, smem,
            pl.BlockSpec((tm * TOK_SUB, LANES), lambda i: (i, 0)),
        ],
        out_specs=pl.BlockSpec(memory_space=pl.ANY),
        scratch_shapes=[pltpu.VMEM((EXP_TILE * TOK_SUB, LANES), u32), pltpu.SemaphoreType.DMA(())],
        compiler_params=_cparams(("arbitrary",), VMEM_LIMIT_MID),
        name="moe_dispatch",
    )(dest_flat, pad_start, pad_len, n_tiles_used, hp)


def _group_start(te_ref, m):
    return jnp.logical_or(m == 0, te_ref[m] != te_ref[jnp.maximum(m - 1, 0)])


def _gate_up_kernel(te_ref, nx_ref, rv_ref, nt_ref, xs_ref, w_hbm, bg_ref, bu_ref, h_ref,
                    wbuf, sems, grp_ref, *, layer):
    j = pl.program_id(0)
    m = pl.program_id(1)
    n_pass = pl.num_programs(0)
    fc = wbuf.shape[3]
    half = EXP_TILE // 2
    valid = m < nt_ref[0]

    def copies(e, jj, slot):
        col = pl.multiple_of(jj * fc, fc)
        return (pltpu.make_async_copy(w_hbm.at[layer, e, :, pl.ds(col, fc)], wbuf.at[slot, 0], sems.at[slot, 0]),
                pltpu.make_async_copy(w_hbm.at[layer, e, :, pl.ds(pl.multiple_of(D_FF + col, fc), fc)],
                                      wbuf.at[slot, 1], sems.at[slot, 1]))

    @pl.when(jnp.logical_and(_group_start(te_ref, m), valid))
    def _():
        @pl.when(jnp.logical_and(j == 0, m == 0))
        def _():
            grp_ref[0] = 0
            for c in copies(te_ref[0], 0, 0):
                c.start()

        g = grp_ref[0]
        slot = g & 1
        for c in copies(te_ref[m], j, slot):
            c.wait()
        nxt = nx_ref[m]

        @pl.when(nxt >= 0)
        def _():
            for c in copies(nxt, j, 1 - slot):
                c.start()

        @pl.when(jnp.logical_and(nxt < 0, j + 1 < n_pass))
        def _():
            for c in copies(te_ref[0], j + 1, 1 - slot):
                c.start()

        grp_ref[0] = g + 1

    cur = (grp_ref[0] + 1) & 1
    dn = (((1,), (0,)), ((), ()))

    def swiglu(x):
        g_lin = lax.dot_general(x, wbuf[cur, 0], dn, preferred_element_type=f32)
        u_lin = lax.dot_general(x, wbuf[cur, 1], dn, preferred_element_type=f32)
        gate = jnp.minimum(g_lin + bg_ref[...], SWIGLU_LIMIT)
        up = jnp.clip(u_lin + bu_ref[...], -SWIGLU_LIMIT, SWIGLU_LIMIT)
        glu = gate * jax.nn.sigmoid(gate * SWIGLU_ALPHA)
        return _pack_bf16_pairs((up + 1.0) * glu)

    @pl.when(rv_ref[m] > half)
    def _():
        h_ref[...] = swiglu(_load_token_tiles_bf16(xs_ref, EXP_TILE))

    @pl.when(jnp.logical_and(rv_ref[m] > 0, rv_ref[m] <= half))
    def _():
        h_ref[pl.ds(0, half), :] = swiglu(_load_token_tiles_bf16(xs_ref, half))
        h_ref[pl.ds(half, half), :] = jnp.zeros((half, h_ref.shape[1]), h_ref.dtype)

    @pl.when(rv_ref[m] == 0)
    def _():
        h_ref[...] = jnp.zeros_like(h_ref)


def _gate_up(xs, w_gate_up, b_gate_up4, layer, tile_e, next_e, rows_valid, n_tiles_used):
    p = xs.shape[0] // TOK_SUB
    d = D_MODEL
    fc = D_FF // 2
    n_pass = D_FF // fc
    tm = EXP_TILE
    grid_spec = pltpu.PrefetchScalarGridSpec(
        num_scalar_prefetch=4,
        grid=(n_pass, p // tm),
        in_specs=[
            pl.BlockSpec((tm * TOK_SUB, LANES), lambda j, m, te, nx, rv, nt: (m, 0)),
            pl.BlockSpec(memory_space=pl.ANY),
            pl.BlockSpec((None, None, 1, fc), lambda j, m, te, nx, rv, nt: (layer, te[m], 0, j)),
            pl.BlockSpec((None, None, 1, fc), lambda j, m, te, nx, rv, nt: (layer, te[m], 0, n_pass + j)),
        ],
        out_specs=pl.BlockSpec((tm, fc // 2), lambda j, m, te, nx, rv, nt: (m, j)),
        scratch_shapes=[
            pltpu.VMEM((2, 2, d, fc), f32),
            pltpu.SemaphoreType.DMA((2, 2)),
            pltpu.SMEM((1,), i32),
        ],
    )
    return pl.pallas_call(
        functools.partial(_gate_up_kernel, layer=layer),
        out_shape=jax.ShapeDtypeStruct((p, D_FF // 2), u32),
        grid_spec=grid_spec,
        compiler_params=_cparams(("arbitrary", "arbitrary"), VMEM_LIMIT_BIG),
        name="moe_gate_up",
    )(tile_e, next_e, rows_valid, n_tiles_used, xs, w_gate_up, b_gate_up4, b_gate_up4)


def _down_kernel(te_ref, nx_ref, rv_ref, nt_ref, h_ref, w_hbm, b_ref, y_ref, wbuf, sems, grp_ref, *, layer):
    m = pl.program_id(0)
    fc = D_FF // 2
    half = EXP_TILE // 2
    valid = m < nt_ref[0]

    def copy(e, slot):
        return pltpu.make_async_copy(w_hbm.at[layer, e], wbuf.at[slot], sems.at[slot])

    @pl.when(jnp.logical_and(_group_start(te_ref, m), valid))
    def _():
        @pl.when(m == 0)
        def _():
            grp_ref[0] = 0
            copy(te_ref[0], 0).start()

        g = grp_ref[0]
        slot = g & 1
        copy(te_ref[m], slot).wait()
        nxt = nx_ref[m]

        @pl.when(nxt >= 0)
        def _():
            copy(nxt, 1 - slot).start()

        grp_ref[0] = g + 1

    cur = (grp_ref[0] + 1) & 1

    def project(hp):
        h = jnp.concatenate([_unpack_bf16_pairs(hp[:, :fc // 2]), _unpack_bf16_pairs(hp[:, fc // 2:])], axis=1)
        y = lax.dot_general(h, wbuf[cur], (((1,), (0,)), ((), ())), preferred_element_type=f32)
        return _pack_bf16_pairs(y + b_ref[...])

    @pl.when(rv_ref[m] > half)
    def _():
        _store_token_tiles(y_ref, project(h_ref[...]))

    @pl.when(jnp.logical_and(rv_ref[m] > 0, rv_ref[m] <= half))
    def _():
        _store_token_tiles(y_ref, project(h_ref[pl.ds(0, half), :]))
        y_ref[pl.ds(half * TOK_SUB, half * TOK_SUB), :] = jnp.zeros((half * TOK_SUB, LANES), y_ref.dtype)

    @pl.when(rv_ref[m] == 0)
    def _():
        y_ref[...] = jnp.zeros_like(y_ref)


def _down(hs, w_down, b_down4, layer, tile_e, next_e, rows_valid, n_tiles_used):
    p, half = hs.shape
    d_ff = 2 * half
    d = w_down.shape[3]
    tm = EXP_TILE
    grid_spec = pltpu.PrefetchScalarGridSpec(
        num_scalar_prefetch=4,
        grid=(p // tm,),
        in_specs=[
            pl.BlockSpec((tm, half), lambda m, te, nx, rv, nt: (m, 0)),
            pl.BlockSpec(memory_space=pl.ANY),
            pl.BlockSpec((None, None, 1, d), lambda m, te, nx, rv, nt: (layer, te[m], 0, 0)),
        ],
        out_specs=pl.BlockSpec((tm * TOK_SUB, LANES), lambda m, te, nx, rv, nt: (m, 0)),
        scratch_shapes=[
            pltpu.VMEM((2, d_ff, d), f32),
            pltpu.SemaphoreType.DMA((2,)),
            pltpu.SMEM((1,), i32),
        ],
    )
    return pl.pallas_call(
        functools.partial(_down_kernel, layer=layer),
        out_shape=jax.ShapeDtypeStruct((p * TOK_SUB, LANES), u32),
        grid_spec=grid_spec,
        compiler_params=_cparams(("arbitrary",), VMEM_LIMIT_BIG),
        name="moe_down",
    )(tile_e, next_e, rows_valid, n_tiles_used, hs, w_down, b_down4)


def _combine_kernel(dest_ref, dest_next_ref, ys_ref, tw_ref, x1_ref, gf_ref, lng_ref, lnb_ref, o_ref,
                    buf_ref, sems):
    tm = x1_ref.shape[0]
    i = pl.program_id(0)
    n = pl.num_programs(0)
    slot = i & 1

    def token_tile(ref, r):
        return ref.at[pl.ds(pl.multiple_of(r * TOK_SUB, TOK_SUB), TOK_SUB), :]

    def copy(dref, s, tok, k):
        return pltpu.make_async_copy(token_tile(ys_ref, dref[tok * TOP_K + k]),
                                     token_tile(buf_ref, (s * TOP_K + k) * tm + tok), sems.at[s])

    def start_all(dref, s):
        def body(tok, c):
            for k in range(TOP_K):
                copy(dref, s, tok, k).start(priority=k % 2)
            return c
        lax.fori_loop(0, tm, body, 0, unroll=2)

    @pl.when(i == 0)
    def _():
        start_all(dest_ref, 0)

    @pl.when(i + 1 < n)
    def _():
        start_all(dest_next_ref, 1 - slot)

    def wait_body(tok, c):
        for k in range(TOP_K):
            copy(dest_ref, slot, tok, k).wait()
        return c

    lax.fori_loop(0, tm, wait_body, 0, unroll=2)

    tw = tw_ref[...]
    y = jnp.zeros((tm, D_MODEL), f32)
    for k in range(TOP_K):
        y = y + tw[:, k:k + 1] * _load_token_tiles_bf16(buf_ref, tm, row0=(slot * TOP_K + k) * tm).astype(f32)
    o_ref[...] = _layer_norm(DEEPNORM_ALPHA * x1_ref[...] + gf_ref[...] * y, lng_ref[...], lnb_ref[...])


def _combine(ys, dest_flat, top_w, x1, mod3, ln_g, ln_b, seq):
    t, d = x1.shape
    tm = TOK_TILE
    tps = seq // tm
    n_steps = t // tm
    grid_spec = pltpu.PrefetchScalarGridSpec(
        num_scalar_prefetch=0,
        grid=(n_steps,),
        in_specs=[
            pl.BlockSpec((tm * TOP_K,), lambda i: (i,), memory_space=pltpu.SMEM),
            pl.BlockSpec((tm * TOP_K,), lambda i: (jnp.minimum(i + 1, n_steps - 1),), memory_space=pltpu.SMEM),
            pl.BlockSpec(memory_space=pl.ANY),
            pl.BlockSpec((tm, TOP_K), lambda i: (i, 0)),
            pl.BlockSpec((tm, d), lambda i: (i, 0)),
            pl.BlockSpec((None, 1, d), lambda i: (i // tps, 0, 5)),
            pl.BlockSpec((1, d), lambda i: (0, 0)),
            pl.BlockSpec((1, d), lambda i: (0, 0)),
        ],
        out_specs=pl.BlockSpec((tm, d), lambda i: (i, 0)),
        scratch_shapes=[pltpu.VMEM((2 * TOP_K * tm * TOK_SUB, LANES), u32), pltpu.SemaphoreType.DMA((2,))],
    )
    return pl.pallas_call(
        _combine_kernel,
        out_shape=jax.ShapeDtypeStruct((t, d), f32),
        grid_spec=grid_spec,
        compiler_params=_cparams(("arbitrary",), VMEM_LIMIT_MID),
        name="moe_combine",
    )(dest_flat, dest_flat, ys, top_w, x1, mod3, ln_g.reshape(1, -1), ln_b.reshape(1, -1))


def _routing_tables(top_e, rank, counts, n_tiles):
    tm = EXP_TILE
    counts = counts.reshape(-1)
    padded = (counts + tm - 1) // tm * tm
    pend = jnp.cumsum(padded)
    pstart = pend - padded
    dest = (pstart[top_e] + rank).reshape(-1).astype(i32)
    tile_start = jnp.arange(n_tiles, dtype=i32) * tm
    tile_e = jnp.sum((pend[None, :] <= tile_start[:, None]).astype(i32), axis=1)
    tile_e = jnp.minimum(tile_e, N_EXPERTS - 1).astype(i32)
    n_used = (pend[-1] // tm).astype(i32).reshape(1)
    mine = jnp.arange(N_EXPERTS, dtype=i32)[None, :] == tile_e[:, None]
    group_end = jnp.sum(jnp.where(mine, pend[None, :], 0), axis=1)
    data_end = jnp.sum(jnp.where(mine, (pstart + counts)[None, :], 0), axis=1)
    rows_valid = jnp.where(tile_start < pend[-1], jnp.clip(data_end - tile_start, 0, tm), 0).astype(i32)
    next_e = jnp.sum((pend[None, :] <= group_end[:, None]).astype(i32), axis=1)
    next_e = jnp.where(group_end < pend[-1], jnp.minimum(next_e, N_EXPERTS - 1), -1).astype(i32)
    return dest, tile_e, next_e, rows_valid, n_used, (pstart + counts).astype(i32), (padded - counts).astype(i32)


def kernel(x, c, rel_bias, w_ada, b_ada, w_in, conv_w, conv_b, lru_w_a, lru_b_a, lru_w_x, lru_b_x, lru_lambda, norm_lru_g, norm_attn_g, w_out, ln_mix_g, ln_mix_b, w_router, b_router, w_gate_up, b_gate_up, w_down, b_down, ln_ffn_g, ln_ffn_b):
    nb, seq, d = x.shape
    t = nb * seq
    mod = _ada_mod(c, w_ada, b_ada)
    b1, b2, b3 = _attn_bias_tables(rel_bias)
    n_tiles = t * TOP_K // EXP_TILE + N_EXPERTS
    q_lo = 2 * D_LRU
    x2 = x.reshape(t, d)
    for l in range(DEPTH):
        mod3 = mod[l].reshape(nb, 1, 6 * d)
        w_in_l = w_in[l].at[:, q_lo:q_lo + D_ATTN].multiply(HEAD_DIM ** -0.5).astype(bf16)
        proj = _in_proj(x2, mod3, w_in_l, seq).reshape(nb, seq, D_IN)
        wg = jnp.concatenate([lru_w_a[l], lru_w_x[l]], axis=-1).astype(bf16)
        bg = jnp.concatenate([lru_b_a[l].reshape(2, N_LRU_BLOCKS, 1, LRU_BLOCK),
                              lru_b_x[l].reshape(2, N_LRU_BLOCKS, 1, LRU_BLOCK)], axis=-1)
        lru = _lru_group(proj, conv_w[l], conv_b[l], wg, bg, lru_lambda[l])
        att = _dilated_attention(proj, b1, b2, b3)
        x1, hp, top_e, top_w, rank, counts = _out_proj_router(
            lru.reshape(t, D_LRU), att.reshape(t, D_ATTN), x2, norm_lru_g[l], norm_attn_g[l],
            w_out[l].astype(bf16), mod3, ln_mix_g[l], ln_mix_b[l], w_router[l], b_router[l], seq)
        dest, tile_e, next_e, rows_valid, n_used, pad_start, pad_len = _routing_tables(top_e, rank, counts, n_tiles)
        xs = _dispatch(hp, dest, pad_start, pad_len, n_used, n_tiles * EXP_TILE)
        hs = _gate_up(xs, w_gate_up, b_gate_up.reshape(DEPTH, N_EXPERTS, 1, 2 * D_FF), l,
                      tile_e, next_e, rows_valid, n_used)
        ys = _down(hs, w_down, b_down.reshape(DEPTH, N_EXPERTS, 1, d), l, tile_e, next_e, rows_valid, n_used)
        x2 = _combine(ys, dest, top_w, x1, mod3, ln_ffn_g[l], ln_ffn_b[l], seq)
    return x2.reshape(nb, seq, d)
```

```python
import functools
import math

import jax
import jax.numpy as jnp
from jax import lax
from jax.experimental import pallas as pl
from jax.experimental.pallas import tpu as pltpu

f32 = jnp.float32
bf16 = jnp.bfloat16
i32 = jnp.int32
u32 = jnp.uint32

D_MODEL = 2048
DEPTH = 2
D_LRU = 1024
N_LRU_BLOCKS = 8
LRU_BLOCK = 128
CONV_WIDTH = 4
LRU_C = 8.0
D_ATTN = 1024
HEAD_DIM = 128
N_HEADS = 8
D_IN = 2 * D_LRU + 3 * D_ATTN
NEG_INF = -1e30
N_BUCKETS = 32
MAX_EXACT = 8
MAX_DISTANCE = 1024
N_EXPERTS = 32
TOP_K = 4
D_FF = D_MODEL
SWIGLU_LIMIT = 7.0
SWIGLU_ALPHA = 1.702
DEEPNORM_ALPHA = (2 * DEPTH) ** 0.25
LN_EPS = 1e-5

VMEM_LIMIT_BIG = 56 * 1024 * 1024
VMEM_LIMIT_MID = 40 * 1024 * 1024

ROW_TILE = 512
ROUTER_TILE = 512
TOK_TILE = 256
EXP_TILE = 512
LRU_CHUNK = 128
Q_BLOCK = 128
NAT_BLOCK = 256
SUB_ROWS = 128
HALF = 64
RES16 = 16
STAGE_PITCH = 24


def _cparams(sem, vmem=None):
    return pltpu.CompilerParams(dimension_semantics=sem, vmem_limit_bytes=vmem)


def _ada_kernel(c_ref, w_ref, b_ref, o_ref):
    c = c_ref[...]
    ca = c * jax.nn.sigmoid(c)
    o_ref[0] = jnp.dot(ca, w_ref[0], precision=lax.Precision.HIGHEST,
                       preferred_element_type=f32) + b_ref[0]


def _ada_mod(c, w_ada, b_ada):
    depth, d, n = w_ada.shape
    b = c.shape[0]
    tn = 1024
    return pl.pallas_call(
        _ada_kernel,
        out_shape=jax.ShapeDtypeStruct((depth, b, n), f32),
        grid=(depth, n // tn),
        in_specs=[
            pl.BlockSpec((b, d), lambda l, j: (0, 0)),
            pl.BlockSpec((1, d, tn), lambda l, j: (l, 0, j)),
            pl.BlockSpec((1, 1, tn), lambda l, j: (l, 0, j)),
        ],
        out_specs=pl.BlockSpec((1, b, tn), lambda l, j: (l, 0, j)),
        compiler_params=_cparams(("parallel", "parallel"), VMEM_LIMIT_MID),
        name="ada_mod",
    )(c, w_ada, b_ada.reshape(depth, 1, n))


def _inproj_kernel(x_ref, sh_ref, sc_ref, w_ref, o_ref):
    h = x_ref[...] * (1.0 + sc_ref[...]) + sh_ref[...]
    o_ref[...] = jnp.dot(h.astype(bf16), w_ref[...], preferred_element_type=f32).astype(o_ref.dtype)


def _in_proj(x2, mod3, w_in_bf, seq):
    t, d = x2.shape
    n = w_in_bf.shape[1]
    tn = n // 2
    tiles_per_seq = seq // ROW_TILE
    return pl.pallas_call(
        _inproj_kernel,
        out_shape=jax.ShapeDtypeStruct((t, n), bf16),
        grid=(n // tn, t // ROW_TILE),
        in_specs=[
            pl.BlockSpec((ROW_TILE, d), lambda j, i: (i, 0)),
            pl.BlockSpec((None, 1, d), lambda j, i: (i // tiles_per_seq, 0, 0)),
            pl.BlockSpec((None, 1, d), lambda j, i: (i // tiles_per_seq, 0, 1)),
            pl.BlockSpec((d, tn), lambda j, i: (0, j)),
        ],
        out_specs=pl.BlockSpec((ROW_TILE, tn), lambda j, i: (i, j)),
        compiler_params=_cparams(("parallel", "parallel"), VMEM_LIMIT_BIG),
        name="in_proj",
    )(x2, mod3, mod3, w_in_bf)


def _lru_kernel(xb_ref, gb_ref, cw_ref, cb_ref, wg_ref, bg_ref, lam_ref, o_ref,
                xt_ref, hs_ref, a_ref, b_ref, hb_ref):
    nb, seq, _ = xb_ref.shape
    tc = LRU_CHUNK
    rows = tc * nb
    n_chunks = seq // tc
    front = (CONV_WIDTH // 2) * nb

    xt_ref[pl.ds(0, front), :] = jnp.zeros((front, LRU_BLOCK), f32)
    xt_ref[pl.ds(front + seq * nb, nb), :] = jnp.zeros((nb, LRU_BLOCK), f32)
    for b in range(nb):
        xt_ref[pl.ds(front + b, seq, stride=nb), :] = xb_ref[b].astype(f32)

    lam = lam_ref[...]
    neg_c_sp = -LRU_C * jax.nn.softplus(-lam)
    cw = cw_ref[...]
    cb = cb_ref[...]

    def gates(d, r0):
        xc = cb
        for j in range(CONV_WIDTH):
            xc = xc + cw[j:j + 1, :] * xt_ref[pl.ds(pl.multiple_of(r0 + j * nb, nb), rows), :]
        g = jnp.dot(xc.astype(bf16), wg_ref[d], preferred_element_type=f32) + bg_ref[d]
        r = 0.5 + 0.5 * jnp.tanh(0.5 * g[:, :LRU_BLOCK])
        i = 0.5 + 0.5 * jnp.tanh(0.5 * g[:, LRU_BLOCK:])
        log_a = neg_c_sp[d:d + 1, :] * r
        a = jnp.exp(log_a)
        u = a * a
        z = 2.0 * log_a
        one_minus = jnp.where(u == 1.0, -z, (1.0 - u) * z / jnp.log(u))
        root = jnp.where(one_minus > 0.0, one_minus * lax.rsqrt(one_minus), 0.0)
        a_ref[...] = a
        b_ref[...] = root * (i * xc)

    def fwd_chunk(c, h):
        r0 = pl.multiple_of(c * rows, rows)
        gates(0, r0)

        def step(t, h):
            rr = pl.multiple_of(t * nb, nb)
            h = a_ref[pl.ds(rr, nb), :] * h + b_ref[pl.ds(rr, nb), :]
            hs_ref[pl.ds(r0 + rr, nb), :] = h
            return h

        return lax.fori_loop(0, tc, step, h, unroll=8)

    lax.fori_loop(0, n_chunks, fwd_chunk, jnp.zeros((nb, LRU_BLOCK), f32))

    def bwd_chunk(ci, h):
        c = n_chunks - 1 - ci
        r0 = pl.multiple_of(c * rows, rows)
        gates(1, r0)

        def step(ti, h):
            rr = pl.multiple_of((tc - 1 - ti) * nb, nb)
            h = a_ref[pl.ds(rr, nb), :] * h + b_ref[pl.ds(rr, nb), :]
            hb_ref[pl.ds(rr, nb), :] = h
            return h

        h = lax.fori_loop(0, tc, step, h, unroll=8)
        hs_ref[pl.ds(r0, rows), :] = hs_ref[pl.ds(r0, rows), :] + hb_ref[...]
        t0 = pl.multiple_of(c * tc, tc)
        for b in range(nb):
            hr = hs_ref[pl.ds(r0 + b, tc, stride=nb), :]
            gate = jax.nn.gelu(gb_ref[b, pl.ds(t0, tc), :].astype(f32))
            o_ref[b, pl.ds(t0, tc), :] = (hr * gate).astype(o_ref.dtype)
        return h

    lax.fori_loop(0, n_chunks, bwd_chunk, jnp.zeros((nb, LRU_BLOCK), f32))


def _lru_group(proj3, conv_w, conv_b, wg, bg, lam):
    nb, seq, _ = proj3.shape
    g = N_LRU_BLOCKS
    return pl.pallas_call(
        _lru_kernel,
        out_shape=jax.ShapeDtypeStruct((nb, seq, D_LRU), bf16),
        grid=(g,),
        in_specs=[
            pl.BlockSpec((nb, seq, LRU_BLOCK), lambda j: (0, 0, j)),
            pl.BlockSpec((nb, seq, LRU_BLOCK), lambda j: (0, 0, g + j)),
            pl.BlockSpec((CONV_WIDTH, LRU_BLOCK), lambda j: (0, j)),
            pl.BlockSpec((1, LRU_BLOCK), lambda j: (0, j)),
            pl.BlockSpec((2, None, LRU_BLOCK, 2 * LRU_BLOCK), lambda j: (0, j, 0, 0)),
            pl.BlockSpec((2, None, 1, 2 * LRU_BLOCK), lambda j: (0, j, 0, 0)),
            pl.BlockSpec((2, LRU_BLOCK), lambda j: (0, j)),
        ],
        out_specs=pl.BlockSpec((nb, seq, LRU_BLOCK), lambda j: (0, 0, j)),
        scratch_shapes=[
            pltpu.VMEM(((seq + CONV_WIDTH - 1) * nb, LRU_BLOCK), f32),
            pltpu.VMEM((seq * nb, LRU_BLOCK), f32),
            pltpu.VMEM((LRU_CHUNK * nb, LRU_BLOCK), f32),
            pltpu.VMEM((LRU_CHUNK * nb, LRU_BLOCK), f32),
            pltpu.VMEM((LRU_CHUNK * nb, LRU_BLOCK), f32),
        ],
        compiler_params=_cparams(("parallel",), VMEM_LIMIT_BIG),
        name="rg_lru",
    )(proj3, proj3, conv_w, conv_b.reshape(1, D_LRU), wg, bg, lam)


def _t5_bucket(rel):
    nbk = N_BUCKETS // 2
    ret = jnp.where(rel > 0, nbk, 0)
    n = jnp.abs(rel)
    nf = jnp.maximum(n, 1).astype(f32)
    large = MAX_EXACT + (jnp.log(nf / MAX_EXACT) / math.log(MAX_DISTANCE / MAX_EXACT)
                         * (nbk - MAX_EXACT)).astype(i32)
    large = jnp.minimum(large, nbk - 1)
    return ret + jnp.where(n < MAX_EXACT, n, large)


def _attn_bias_tables(rel_bias):
    qb = Q_BLOCK
    rb = rel_bias.astype(f32)
    i = jnp.arange(qb, dtype=i32)[:, None]

    def tile(step, dilation, valid):
        onehot = jax.nn.one_hot(_t5_bucket(step * dilation), N_BUCKETS, dtype=f32)
        t = jnp.einsum("qkn,nh->hqk", onehot, rb, precision=lax.Precision.HIGHEST)
        return jnp.where(valid[None], t, NEG_INF)

    i1 = jnp.arange(NAT_BLOCK, dtype=i32)[:, None]
    win = NAT_BLOCK + 2 * HALF
    j = jnp.arange(win, dtype=i32)[None, :]
    step1 = j - HALF - i1
    band1 = jnp.abs(step1) <= HALF
    b1 = jnp.stack([tile(step1, 1, band1),
                    tile(step1, 1, band1 & (j >= HALF)),
                    tile(step1, 1, band1 & (j < win - HALF))],
                   axis=1)
    lk = jnp.arange(qb, dtype=i32)[None, :]
    step2 = jnp.concatenate(
        [jnp.concatenate([4 * (lk - i) + (jk - jq) for jk in range(4)], axis=1) for jq in range(4)], axis=0)
    b2 = tile(step2, 4, jnp.abs(step2) <= HALF)
    step3 = lk - i
    b3 = tile(step3, 16, jnp.abs(step3) <= HALF)
    return b1, b2, b3


def _lane_tiles(a):
    return [a[:, c:c + LANES] for c in range(0, a.shape[1], LANES)] if a.shape[1] % LANES == 0 else [a]


def _softmax_numer(s):
    m = jnp.max(functools.reduce(jnp.maximum, _lane_tiles(s)), axis=-1, keepdims=True)
    return m, jnp.exp(s - m).astype(bf16)


def _pv_and_rowsum(p, v_aug):
    o = jnp.dot(p, v_aug, preferred_element_type=f32)
    return o[:, :HEAD_DIM], o[:, HEAD_DIM:]


def _attn_kernel(q_ref, k_ref, v_ref, b1_ref, b2_ref, b3_ref, o_ref,
                 stage_ref, qp_ref, kp_ref, vp_ref, kpad_ref, vpad_ref, o23_ref, l23_ref):
    seq = q_ref.shape[0]
    qb = NAT_BLOCK
    nblk = seq // qb
    per_res = seq // RES16
    cls = 4 * per_res
    dn = (((1,), (1,)), ((), ()))

    def perm_row0(r16):
        return ((r16 % 4) * 4 + r16 // 4) * per_res

    def pitched(ref, g, rows=RES16):
        return ref.at[pl.ds(pl.multiple_of(g * STAGE_PITCH, 8), rows), :]

    for src, dst in ((q_ref, qp_ref), (k_ref, kp_ref), (v_ref, vp_ref)):
        def fill(g, c, src=src):
            pitched(stage_ref, g)[...] = src[pl.ds(pl.multiple_of(g * RES16, RES16), RES16), :].astype(f32)
            return c

        lax.fori_loop(0, per_res, fill, 0, unroll=8)
        for r in range(RES16):
            dst[pl.ds(perm_row0(r), per_res), pl.ds(0, HEAD_DIM)] = (
                stage_ref[pl.ds(r, per_res, stride=STAGE_PITCH), :].astype(bf16))

    head = pl.ds(0, HEAD_DIM)
    ones_cols = pl.ds(HEAD_DIM, HEAD_DIM)
    zpad = jnp.zeros((HALF, HEAD_DIM), bf16)
    kpad_ref[pl.ds(0, HALF), :] = zpad
    vpad_ref[pl.ds(0, HALF), head] = zpad
    kpad_ref[pl.ds(HALF + seq, HALF), :] = zpad
    vpad_ref[pl.ds(HALF + seq, HALF), head] = zpad
    kpad_ref[pl.ds(HALF, seq), :] = k_ref[...]
    vpad_ref[pl.ds(HALF, seq), head] = v_ref[...]
    vpad_ref[:, ones_cols] = jnp.ones((seq + 2 * HALF, HEAD_DIM), bf16)
    vp_ref[:, ones_cols] = jnp.ones((seq, HEAD_DIM), bf16)

    for r4 in range(4):
        base = r4 * cls
        q4 = qp_ref[pl.ds(base, cls), :]
        k4 = kp_ref[pl.ds(base, cls), :]
        v4 = vp_ref[pl.ds(base, cls), :]
        raw = lax.dot_general(q4, k4, dn, preferred_element_type=f32)
        m2, p2 = _softmax_numer(raw + b2_ref[...])
        o2, d2 = _pv_and_rowsum(p2, v4)
        m3s, d3s, o3s = [], [], []
        for jq in range(4):
            lo, hi = jq * per_res, (jq + 1) * per_res
            m3, p3 = _softmax_numer(raw[lo:hi, lo:hi] + b3_ref[...])
            o3, d3 = _pv_and_rowsum(p3, v4[lo:hi])
            m3s.append(m3)
            d3s.append(d3)
            o3s.append(o3)
        m3 = jnp.concatenate(m3s, axis=0)
        d3 = jnp.concatenate(d3s, axis=0)
        o3 = jnp.concatenate(o3s, axis=0)
        m23 = jnp.maximum(m2, m3)
        w2 = jnp.exp(m2 - m23)
        w3 = jnp.exp(m3 - m23)
        d23 = w2 * d2 + w3 * d3
        o23_ref[pl.ds(base, cls), :] = (w2 * o2 + w3 * o3) / d23
        l23_ref[pl.ds(base, cls), :] = m23 + jnp.log(d23)

    for r in range(RES16):
        stage_ref[pl.ds(r, per_res, stride=STAGE_PITCH), :] = o23_ref[pl.ds(perm_row0(r), per_res), :]
    for r in range(RES16):
        o23_ref[pl.ds(r, per_res, stride=STAGE_PITCH), :] = l23_ref[pl.ds(perm_row0(r), per_res), :]

    def natural_rows(ref, g0, rows):
        return jnp.concatenate([pitched(ref, g0 + i)[...] for i in range(rows // RES16)], axis=0)

    per_group = 8
    n_groups = nblk // per_group

    def nat_group(gi, carry):
        for sb in range(per_group):
            row0 = pl.multiple_of((gi * per_group + sb) * qb, qb)
            q = q_ref[pl.ds(row0, qb), :]
            kw = kpad_ref[pl.ds(row0, qb + 2 * HALF), :]
            vw = vpad_ref[pl.ds(row0, qb + 2 * HALF), :]
            if sb == 0:
                bias = b1_ref[jnp.where(gi == 0, 1, 0)]
            elif sb == per_group - 1:
                bias = b1_ref[jnp.where(gi == n_groups - 1, 2, 0)]
            else:
                bias = b1_ref[0]
            s1 = lax.dot_general(q, kw, dn, preferred_element_type=f32) + bias
            m1, p1 = _softmax_numer(s1)
            o1, d1 = _pv_and_rowsum(p1, vw)
            g0 = (gi * per_group + sb) * (qb // RES16)
            l23 = natural_rows(o23_ref, g0, qb)
            on23 = natural_rows(stage_ref, g0, qb)
            mm = jnp.maximum(m1, l23)
            w1 = jnp.exp(m1 - mm)
            w23 = jnp.exp(l23 - mm)
            o_ref[pl.ds(row0, qb), :] = ((w1 * o1 + w23 * on23) / (w1 * d1 + w23)).astype(o_ref.dtype)
        return carry

    lax.fori_loop(0, n_groups, nat_group, 0)


def _dilated_attention(proj3, b1, b2, b3):
    nb, seq, _ = proj3.shape
    assert seq // RES16 == Q_BLOCK and seq % (8 * NAT_BLOCK) == 0
    q0 = 2 * D_LRU // HEAD_DIM
    k0 = q0 + N_HEADS
    v0 = k0 + N_HEADS
    return pl.pallas_call(
        _attn_kernel,
        out_shape=jax.ShapeDtypeStruct((nb, seq, D_ATTN), bf16),
        grid=(N_HEADS, nb),
        in_specs=[
            pl.BlockSpec((None, seq, HEAD_DIM), lambda h, b: (b, 0, q0 + h)),
            pl.BlockSpec((None, seq, HEAD_DIM), lambda h, b: (b, 0, k0 + h)),
            pl.BlockSpec((None, seq, HEAD_DIM), lambda h, b: (b, 0, v0 + h)),
            pl.BlockSpec((None, 3, NAT_BLOCK, NAT_BLOCK + 2 * HALF), lambda h, b: (h, 0, 0, 0)),
            pl.BlockSpec((None, 4 * Q_BLOCK, 4 * Q_BLOCK), lambda h, b: (h, 0, 0)),
            pl.BlockSpec((None, Q_BLOCK, Q_BLOCK), lambda h, b: (h, 0, 0)),
        ],
        out_specs=pl.BlockSpec((None, seq, HEAD_DIM), lambda h, b: (b, 0, h)),
        scratch_shapes=[
            pltpu.VMEM((seq // RES16 * STAGE_PITCH, HEAD_DIM), f32),
            pltpu.VMEM((seq, HEAD_DIM), bf16),
            pltpu.VMEM((seq, HEAD_DIM), bf16),
            pltpu.VMEM((seq, 2 * HEAD_DIM), bf16),
            pltpu.VMEM((seq + 2 * HALF, HEAD_DIM), bf16),
            pltpu.VMEM((seq + 2 * HALF, 2 * HEAD_DIM), bf16),
            pltpu.VMEM((seq // RES16 * STAGE_PITCH, HEAD_DIM), f32),
            pltpu.VMEM((seq, HEAD_DIM), f32),
        ],
        compiler_params=_cparams(("parallel", "parallel"), VMEM_LIMIT_MID),
        name="dilated_attn",
    )(proj3, proj3, proj3, b1, b2, b3)


def _layer_norm(z, g, b):
    mu = jnp.mean(z, axis=-1, keepdims=True)
    zc = z - mu
    var = jnp.mean(zc * zc, axis=-1, keepdims=True)
    return zc * lax.rsqrt(var + LN_EPS) * g + b


def _rms_norm(v, g):
    return v * lax.rsqrt(jnp.mean(v * v, axis=-1, keepdims=True) + LN_EPS) * g


def _pack_bf16_pairs(v):
    n = v.shape[1] // 2
    lo = pltpu.bitcast(v[:, :n].astype(bf16).astype(f32), u32)
    hi = pltpu.bitcast(v[:, n:].astype(bf16).astype(f32), u32)
    return (lo >> 16) | (hi & jnp.uint32(0xFFFF0000))


def _unpack_bf16_pairs(p):
    lo = pltpu.bitcast(p << 16, f32)
    hi = pltpu.bitcast(p & jnp.uint32(0xFFFF0000), f32)
    return jnp.concatenate([lo.astype(bf16), hi.astype(bf16)], axis=1)


LANES = 128
TOK_SUB = D_MODEL // 2 // LANES


def _store_token_tiles(ref, packed, row0=0):
    rows = packed.shape[0]
    for c in range(TOK_SUB):
        ref[pl.ds(TOK_SUB * row0 + c, rows, stride=TOK_SUB), :] = packed[:, c * LANES:(c + 1) * LANES]


def _load_token_tiles_bf16(ref, rows, row0=0):
    los, his = [], []
    for c in range(TOK_SUB):
        p = ref[pl.ds(TOK_SUB * row0 + c, rows, stride=TOK_SUB), :]
        los.append(pltpu.bitcast(p << 16, f32).astype(bf16))
        his.append(pltpu.bitcast(p & jnp.uint32(0xFFFF0000), f32).astype(bf16))
    return jnp.concatenate(los + his, axis=1)


def _outproj_kernel(lru_ref, att_ref, x_ref, gl_ref, ga_ref, w_ref, gm_ref, shf_ref, scf_ref,
                    lng_ref, lnb_ref, wrc_ref, br_ref,
                    x1_ref, hp_ref, te_ref, tw_ref, rk_ref, cnt_ref, carry_ref):
    tm = x_ref.shape[0]
    step = pl.program_id(0)

    @pl.when(step == 0)
    def _():
        carry_ref[...] = jnp.zeros_like(carry_ref)

    ln = _rms_norm(lru_ref[...].astype(f32), gl_ref[...]).astype(bf16)
    an = _rms_norm(att_ref[...].astype(f32), ga_ref[...]).astype(bf16)
    y = (jnp.dot(ln, w_ref[pl.ds(0, D_LRU), :], preferred_element_type=f32)
         + jnp.dot(an, w_ref[pl.ds(D_LRU, D_ATTN), :], preferred_element_type=f32))
    x1 = _layer_norm(DEEPNORM_ALPHA * x_ref[...] + gm_ref[...] * y, lng_ref[...], lnb_ref[...])
    x1_ref[...] = x1
    hf = x1 * (1.0 + scf_ref[...]) + shf_ref[...]
    _store_token_tiles(hp_ref, _pack_bf16_pairs(hf))

    hf_hi = hf.astype(bf16)
    hf_lo = (hf - hf_hi.astype(f32)).astype(bf16)
    hi_both = jnp.dot(hf_hi, wrc_ref[...], preferred_element_type=f32)
    lo_hi = jnp.dot(hf_lo, wrc_ref[:, :N_EXPERTS], preferred_element_type=f32)
    logits = hi_both[:, :N_EXPERTS] + (lo_hi + hi_both[:, N_EXPERTS:]) + br_ref[...]
    lane = lax.broadcasted_iota(i32, (tm, N_EXPERTS), 1)
    col4 = lax.broadcasted_iota(i32, (tm, TOP_K), 1)
    work = logits
    vals, idxs = [], []
    for _k in range(TOP_K):
        m = jnp.max(work, axis=-1, keepdims=True)
        idx = jnp.min(jnp.where(work == m, lane, N_EXPERTS), axis=-1, keepdims=True)
        vals.append(m)
        idxs.append(idx)
        work = jnp.where(lane == idx, -jnp.inf, work)
    exps = [jnp.exp(v - vals[0]) for v in vals]
    den = exps[0] + exps[1] + exps[2] + exps[3]

    onehot = jnp.zeros((tm, N_EXPERTS), f32)
    for idx in idxs:
        onehot = onehot + (lane == idx).astype(f32)
    ri = lax.broadcasted_iota(i32, (tm, tm), 0)
    ci = lax.broadcasted_iota(i32, (tm, tm), 1)
    lower = (ri > ci).astype(bf16)
    before = jnp.dot(lower, onehot.astype(bf16), preferred_element_type=f32) + carry_ref[...]
    te = jnp.zeros((tm, TOP_K), i32)
    tw = jnp.zeros((tm, TOP_K), f32)
    rk = jnp.zeros((tm, TOP_K), i32)
    for k in range(TOP_K):
        rank_k = jnp.sum(jnp.where(lane == idxs[k], before, 0.0), axis=-1, keepdims=True)
        te = jnp.where(col4 == k, idxs[k], te)
        tw = jnp.where(col4 == k, exps[k] / den, tw)
        rk = jnp.where(col4 == k, rank_k.astype(i32), rk)
    te_ref[...] = te
    tw_ref[...] = tw
    rk_ref[...] = rk
    carry_ref[...] = carry_ref[...] + jnp.sum(onehot, axis=0, keepdims=True)
    cnt_ref[...] = carry_ref[...].astype(i32)


def _out_proj_router(lru2, att2, x2, g_lru, g_attn, w_out_bf, mod3, ln_g, ln_b, w_router, b_router, seq):
    t, d = x2.shape
    tm = ROUTER_TILE
    tps = seq // tm
    row = lambda i: (i, 0)
    const = lambda i: (0, 0)
    wr_hi = w_router.astype(bf16)
    wr_cat = jnp.concatenate([wr_hi, (w_router - wr_hi.astype(f32)).astype(bf16)], axis=1)
    return pl.pallas_call(
        _outproj_kernel,
        out_shape=(
            jax.ShapeDtypeStruct((t, d), f32),
            jax.ShapeDtypeStruct((t * TOK_SUB, LANES), u32),
            jax.ShapeDtypeStruct((t, TOP_K), i32),
            jax.ShapeDtypeStruct((t, TOP_K), f32),
            jax.ShapeDtypeStruct((t, TOP_K), i32),
            jax.ShapeDtypeStruct((1, N_EXPERTS), i32),
        ),
        grid=(t // tm,),
        in_specs=[
            pl.BlockSpec((tm, D_LRU), row),
            pl.BlockSpec((tm, D_ATTN), row),
            pl.BlockSpec((tm, d), row),
            pl.BlockSpec((1, D_LRU), const),
            pl.BlockSpec((1, D_ATTN), const),
            pl.BlockSpec((d, d), const),
            pl.BlockSpec((None, 1, d), lambda i: (i // tps, 0, 2)),
            pl.BlockSpec((None, 1, d), lambda i: (i // tps, 0, 3)),
            pl.BlockSpec((None, 1, d), lambda i: (i // tps, 0, 4)),
            pl.BlockSpec((1, d), const),
            pl.BlockSpec((1, d), const),
            pl.BlockSpec((d, 2 * N_EXPERTS), const),
            pl.BlockSpec((1, N_EXPERTS), const),
        ],
        out_specs=(
            pl.BlockSpec((tm, d), row),
            pl.BlockSpec((tm * TOK_SUB, LANES), row),
            pl.BlockSpec((tm, TOP_K), row),
            pl.BlockSpec((tm, TOP_K), row),
            pl.BlockSpec((tm, TOP_K), row),
            pl.BlockSpec((1, N_EXPERTS), const),
        ),
        scratch_shapes=[pltpu.VMEM((1, N_EXPERTS), f32)],
        compiler_params=_cparams(("arbitrary",), VMEM_LIMIT_BIG),
        name="out_proj_router",
    )(lru2, att2, x2, g_lru.reshape(1, -1), g_attn.reshape(1, -1), w_out_bf, mod3, mod3, mod3,
      ln_g.reshape(1, -1), ln_b.reshape(1, -1), wr_cat, b_router.reshape(1, -1))


def _dispatch_kernel(dest_ref, pad_start_ref, pad_len_ref, nt_ref, hp_ref, xs_ref, zero_ref, sem):
    tm = hp_ref.shape[0] // TOK_SUB
    tile_rows = EXP_TILE * TOK_SUB
    n_tiles = xs_ref.shape[0] // tile_rows

    def token_tile(ref, r):
        return ref.at[pl.ds(pl.multiple_of(r * TOK_SUB, TOK_SUB), TOK_SUB), :]

    def copy(tok, k):
        return pltpu.make_async_copy(token_tile(hp_ref, tok), token_tile(xs_ref, dest_ref[tok * TOP_K + k]), sem)

    def start(tok, c):
        for k in range(TOP_K):
            copy(tok, k).start(priority=k % 2)
        return c

    def wait(tok, c):
        for k in range(TOP_K):
            copy(tok, k).wait()
        return c

    lax.fori_loop(0, tm, start, 0, unroll=2)
    lax.fori_loop(0, tm, wait, 0, unroll=2)

    @pl.when(pl.program_id(0) == pl.num_programs(0) - 1)
    def _():
        zero_ref[...] = jnp.zeros_like(zero_ref)

        def per_expert(e, c):
            p0 = pad_start_ref[e]
            n = pad_len_ref[e]

            def zcopy(r):
                return pltpu.make_async_copy(token_tile(zero_ref, 0), token_tile(xs_ref, p0 + r), sem)

            lax.fori_loop(0, n, lambda r, c2: (zcopy(r).start(), c2)[1], 0)
            lax.fori_loop(0, n, lambda r, c2: (zcopy(r).wait(), c2)[1], 0)
            return c

        lax.fori_loop(0, N_EXPERTS, per_expert, 0)

        def tail_tile(m, c):
            cp = pltpu.make_async_copy(
                zero_ref, xs_ref.at[pl.ds(pl.multiple_of(m * tile_rows, tile_rows), tile_rows), :], sem)
            cp.start()
            cp.wait()
            return c

        lax.fori_loop(nt_ref[0], n_tiles, tail_tile, 0)


def _dispatch(hp, dest_flat, pad_start, pad_len, n_tiles_used, n_rows):
    t = hp.shape[0] // TOK_SUB
    tm = TOK_TILE
    smem = pl.BlockSpec(memory_space=pltpu.SMEM)
    return pl.pallas_call(
        _dispatch_kernel,
        out_shape=jax.ShapeDtypeStruct((n_rows * TOK_SUB, LANES), u32),
        grid=(t // tm,),
        in_specs=[
            pl.BlockSpec((tm * TOP_K,), lambda i: (i,), memory_space=pltpu.SMEM),
            smem, smem, smem,
            pl.BlockSpec((tm * TOK_SUB, LANES), lambda i: (i, 0)),
        ],
        out_specs=pl.BlockSpec(memory_space=pl.ANY),
        scratch_shapes=[pltpu.VMEM((EXP_TILE * TOK_SUB, LANES), u32), pltpu.SemaphoreType.DMA(())],
        compiler_params=_cparams(("arbitrary",), VMEM_LIMIT_MID),
        name="moe_dispatch",
    )(dest_flat, pad_start, pad_len, n_tiles_used, hp)


def _group_start(te_ref, m):
    return jnp.logical_or(m == 0, te_ref[m] != te_ref[jnp.maximum(m - 1, 0)])


def _gate_up_kernel(te_ref, nx_ref, rv_ref, nt_ref, xs_ref, w_hbm, bg_ref, bu_ref, h_ref,
                    wbuf, sems, grp_ref, *, layer):
    j = pl.program_id(0)
    m = pl.program_id(1)
    n_pass = pl.num_programs(0)
    fc = wbuf.shape[3]
    half = EXP_TILE // 2
    valid = m < nt_ref[0]

    def copies(e, jj, slot):
        col = pl.multiple_of(jj * fc, fc)
        return (pltpu.make_async_copy(w_hbm.at[layer, e, :, pl.ds(col, fc)], wbuf.at[slot, 0], sems.at[slot, 0]),
                pltpu.make_async_copy(w_hbm.at[layer, e, :, pl.ds(pl.multiple_of(D_FF + col, fc), fc)],
                                      wbuf.at[slot, 1], sems.at[slot, 1]))

    @pl.when(jnp.logical_and(_group_start(te_ref, m), valid))
    def _():
        @pl.when(jnp.logical_and(j == 0, m == 0))
        def _():
            grp_ref[0] = 0
            for c in copies(te_ref[0], 0, 0):
                c.start()

        g = grp_ref[0]
        slot = g & 1
        for c in copies(te_ref[m], j, slot):
            c.wait()
        nxt = nx_ref[m]

        @pl.when(nxt >= 0)
        def _():
            for c in copies(nxt, j, 1 - slot):
                c.start()

        @pl.when(jnp.logical_and(nxt < 0, j + 1 < n_pass))
        def _():
            for c in copies(te_ref[0], j + 1, 1 - slot):
                c.start()

        grp_ref[0] = g + 1

    cur = (grp_ref[0] + 1) & 1
    dn = (((1,), (0,)), ((), ()))

    def swiglu(x):
        g_lin = lax.dot_general(x, wbuf[cur, 0], dn, preferred_element_type=f32)
        u_lin = lax.dot_general(x, wbuf[cur, 1], dn, preferred_element_type=f32)
        gate = jnp.minimum(g_lin + bg_ref[...], SWIGLU_LIMIT)
        up = jnp.clip(u_lin + bu_ref[...], -SWIGLU_LIMIT, SWIGLU_LIMIT)
        glu = gate * jax.nn.sigmoid(gate * SWIGLU_ALPHA)
        return _pack_bf16_pairs((up + 1.0) * glu)

    @pl.when(rv_ref[m] > half)
    def _():
        h_ref[...] = swiglu(_load_token_tiles_bf16(xs_ref, EXP_TILE))

    @pl.when(jnp.logical_and(rv_ref[m] > 0, rv_ref[m] <= half))
    def _():
        h_ref[pl.ds(0, half), :] = swiglu(_load_token_tiles_bf16(xs_ref, half))
        h_ref[pl.ds(half, half), :] = jnp.zeros((half, h_ref.shape[1]), h_ref.dtype)

    @pl.when(rv_ref[m] == 0)
    def _():
        h_ref[...] = jnp.zeros_like(h_ref)


def _gate_up(xs, w_gate_up, b_gate_up4, layer, tile_e, next_e, rows_valid, n_tiles_used):
    p = xs.shape[0] // TOK_SUB
    d = D_MODEL
    fc = D_FF // 2
    n_pass = D_FF // fc
    tm = EXP_TILE
    grid_spec = pltpu.PrefetchScalarGridSpec(
        num_scalar_prefetch=4,
        grid=(n_pass, p // tm),
        in_specs=[
            pl.BlockSpec((tm * TOK_SUB, LANES), lambda j, m, te, nx, rv, nt: (m, 0)),
            pl.BlockSpec(memory_space=pl.ANY),
            pl.BlockSpec((None, None, 1, fc), lambda j, m, te, nx, rv, nt: (layer, te[m], 0, j)),
            pl.BlockSpec((None, None, 1, fc), lambda j, m, te, nx, rv, nt: (layer, te[m], 0, n_pass + j)),
        ],
        out_specs=pl.BlockSpec((tm, fc // 2), lambda j, m, te, nx, rv, nt: (m, j)),
        scratch_shapes=[
            pltpu.VMEM((2, 2, d, fc), f32),
            pltpu.SemaphoreType.DMA((2, 2)),
            pltpu.SMEM((1,), i32),
        ],
    )
    return pl.pallas_call(
        functools.partial(_gate_up_kernel, layer=layer),
        out_shape=jax.ShapeDtypeStruct((p, D_FF // 2), u32),
        grid_spec=grid_spec,
        compiler_params=_cparams(("arbitrary", "arbitrary"), VMEM_LIMIT_BIG),
        name="moe_gate_up",
    )(tile_e, next_e, rows_valid, n_tiles_used, xs, w_gate_up, b_gate_up4, b_gate_up4)


def _down_kernel(te_ref, nx_ref, rv_ref, nt_ref, h_ref, w_hbm, b_ref, y_ref, wbuf, sems, grp_ref, *, layer):
    m = pl.program_id(0)
    fc = D_FF // 2
    half = EXP_TILE // 2
    valid = m < nt_ref[0]

    def copy(e, slot):
        return pltpu.make_async_copy(w_hbm.at[layer, e], wbuf.at[slot], sems.at[slot])

    @pl.when(jnp.logical_and(_group_start(te_ref, m), valid))
    def _():
        @pl.when(m == 0)
        def _():
            grp_ref[0] = 0
            copy(te_ref[0], 0).start()

        g = grp_ref[0]
        slot = g & 1
        copy(te_ref[m], slot).wait()
        nxt = nx_ref[m]

        @pl.when(nxt >= 0)
        def _():
            copy(nxt, 1 - slot).start()

        grp_ref[0] = g + 1

    cur = (grp_ref[0] + 1) & 1

    def project(hp):
        h = jnp.concatenate([_unpack_bf16_pairs(hp[:, :fc // 2]), _unpack_bf16_pairs(hp[:, fc // 2:])], axis=1)
        y = lax.dot_general(h, wbuf[cur], (((1,), (0,)), ((), ())), preferred_element_type=f32)
        return _pack_bf16_pairs(y + b_ref[...])

    @pl.when(rv_ref[m] > half)
    def _():
        _store_token_tiles(y_ref, project(h_ref[...]))

    @pl.when(jnp.logical_and(rv_ref[m] > 0, rv_ref[m] <= half))
    def _():
        _store_token_tiles(y_ref, project(h_ref[pl.ds(0, half), :]))
        y_ref[pl.ds(half * TOK_SUB, half * TOK_SUB), :] = jnp.zeros((half * TOK_SUB, LANES), y_ref.dtype)

    @pl.when(rv_ref[m] == 0)
    def _():
        y_ref[...] = jnp.zeros_like(y_ref)


def _down(hs, w_down, b_down4, layer, tile_e, next_e, rows_valid, n_tiles_used):
    p, half = hs.shape
    d_ff = 2 * half
    d = w_down.shape[3]
    tm = EXP_TILE
    grid_spec = pltpu.PrefetchScalarGridSpec(
        num_scalar_prefetch=4,
        grid=(p // tm,),
        in_specs=[
            pl.BlockSpec((tm, half), lambda m, te, nx, rv, nt: (m, 0)),
            pl.BlockSpec(memory_space=pl.ANY),
            pl.BlockSpec((None, None, 1, d), lambda m, te, nx, rv, nt: (layer, te[m], 0, 0)),
        ],
        out_specs=pl.BlockSpec((tm * TOK_SUB, LANES), lambda m, te, nx, rv, nt: (m, 0)),
        scratch_shapes=[
            pltpu.VMEM((2, d_ff, d), f32),
            pltpu.SemaphoreType.DMA((2,)),
            pltpu.SMEM((1,), i32),
        ],
    )
    return pl.pallas_call(
        functools.partial(_down_kernel, layer=layer),
        out_shape=jax.ShapeDtypeStruct((p * TOK_SUB, LANES), u32),
        grid_spec=grid_spec,
        compiler_params=_cparams(("arbitrary",), VMEM_LIMIT_BIG),
        name="moe_down",
    )(tile_e, next_e, rows_valid, n_tiles_used, hs, w_down, b_down4)


def _combine_kernel(dest_ref, dest_next_ref, ys_ref, tw_ref, x1_ref, gf_ref, lng_ref, lnb_ref, o_ref,
                    buf_ref, sems):
    tm = x1_ref.shape[0]
    i = pl.program_id(0)
    n = pl.num_programs(0)
    slot = i & 1

    def token_tile(ref, r):
        return ref.at[pl.ds(pl.multiple_of(r * TOK_SUB, TOK_SUB), TOK_SUB), :]

    def copy(dref, s, tok, k):
        return pltpu.make_async_copy(token_tile(ys_ref, dref[tok * TOP_K + k]),
                                     token_tile(buf_ref, (s * TOP_K + k) * tm + tok), sems.at[s])

    def start_all(dref, s):
        def body(tok, c):
            for k in range(TOP_K):
                copy(dref, s, tok, k).start(priority=k % 2)
            return c
        lax.fori_loop(0, tm, body, 0, unroll=2)

    @pl.when(i == 0)
    def _():
        start_all(dest_ref, 0)

    @pl.when(i + 1 < n)
    def _():
        start_all(dest_next_ref, 1 - slot)

    def wait_body(tok, c):
        for k in range(TOP_K):
            copy(dest_ref, slot, tok, k).wait()
        return c

    lax.fori_loop(0, tm, wait_body, 0, unroll=2)

    tw = tw_ref[...]
    y = jnp.zeros((tm, D_MODEL), f32)
    for k in range(TOP_K):
        y = y + tw[:, k:k + 1] * _load_token_tiles_bf16(buf_ref, tm, row0=(slot * TOP_K + k) * tm).astype(f32)
    o_ref[...] = _layer_norm(DEEPNORM_ALPHA * x1_ref[...] + gf_ref[...] * y, lng_ref[...], lnb_ref[...])


def _combine(ys, dest_flat, top_w, x1, mod3, ln_g, ln_b, seq):
    t, d = x1.shape
    tm = TOK_TILE
    tps = seq // tm
    n_steps = t // tm
    grid_spec = pltpu.PrefetchScalarGridSpec(
        num_scalar_prefetch=0,
        grid=(n_steps,),
        in_specs=[
            pl.BlockSpec((tm * TOP_K,), lambda i: (i,), memory_space=pltpu.SMEM),
            pl.BlockSpec((tm * TOP_K,), lambda i: (jnp.minimum(i + 1, n_steps - 1),), memory_space=pltpu.SMEM),
            pl.BlockSpec(memory_space=pl.ANY),
            pl.BlockSpec((tm, TOP_K), lambda i: (i, 0)),
            pl.BlockSpec((tm, d), lambda i: (i, 0)),
            pl.BlockSpec((None, 1, d), lambda i: (i // tps, 0, 5)),
            pl.BlockSpec((1, d), lambda i: (0, 0)),
            pl.BlockSpec((1, d), lambda i: (0, 0)),
        ],
        out_specs=pl.BlockSpec((tm, d), lambda i: (i, 0)),
        scratch_shapes=[pltpu.VMEM((2 * TOP_K * tm * TOK_SUB, LANES), u32), pltpu.SemaphoreType.DMA((2,))],
    )
    return pl.pallas_call(
        _combine_kernel,
        out_shape=jax.ShapeDtypeStruct((t, d), f32),
        grid_spec=grid_spec,
        compiler_params=_cparams(("arbitrary",), VMEM_LIMIT_MID),
        name="moe_combine",
    )(dest_flat, dest_flat, ys, top_w, x1, mod3, ln_g.reshape(1, -1), ln_b.reshape(1, -1))


def _routing_tables(top_e, rank, counts, n_tiles):
    tm = EXP_TILE
    counts = counts.reshape(-1)
    padded = (counts + tm - 1) // tm * tm
    pend = jnp.cumsum(padded)
    pstart = pend - padded
    dest = (pstart[top_e] + rank).reshape(-1).astype(i32)
    tile_start = jnp.arange(n_tiles, dtype=i32) * tm
    tile_e = jnp.sum((pend[None, :] <= tile_start[:, None]).astype(i32), axis=1)
    tile_e = jnp.minimum(tile_e, N_EXPERTS - 1).astype(i32)
    n_used = (pend[-1] // tm).astype(i32).reshape(1)
    mine = jnp.arange(N_EXPERTS, dtype=i32)[None, :] == tile_e[:, None]
    group_end = jnp.sum(jnp.where(mine, pend[None, :], 0), axis=1)
    data_end = jnp.sum(jnp.where(mine, (pstart + counts)[None, :], 0), axis=1)
    rows_valid = jnp.where(tile_start < pend[-1], jnp.clip(data_end - tile_start, 0, tm), 0).astype(i32)
    next_e = jnp.sum((pend[None, :] <= group_end[:, None]).astype(i32), axis=1)
    next_e = jnp.where(group_end < pend[-1], jnp.minimum(next_e, N_EXPERTS - 1), -1).astype(i32)
    return dest, tile_e, next_e, rows_valid, n_used, (pstart + counts).astype(i32), (padded - counts).astype(i32)


def kernel(x, c, rel_bias, w_ada, b_ada, w_in, conv_w, conv_b, lru_w_a, lru_b_a, lru_w_x, lru_b_x, lru_lambda, norm_lru_g, norm_attn_g, w_out, ln_mix_g, ln_mix_b, w_router, b_router, w_gate_up, b_gate_up, w_down, b_down, ln_ffn_g, ln_ffn_b):
    nb, seq, d = x.shape
    t = nb * seq
    mod = _ada_mod(c, w_ada, b_ada)
    b1, b2, b3 = _attn_bias_tables(rel_bias)
    n_tiles = t * TOP_K // EXP_TILE + N_EXPERTS
    q_lo = 2 * D_LRU
    x2 = x.reshape(t, d)
    for l in range(DEPTH):
        mod3 = mod[l].reshape(nb, 1, 6 * d)
        w_in_l = w_in[l].at[:, q_lo:q_lo + D_ATTN].multiply(HEAD_DIM ** -0.5).astype(bf16)
        proj = _in_proj(x2, mod3, w_in_l, seq).reshape(nb, seq, D_IN)
        wg = jnp.concatenate([lru_w_a[l], lru_w_x[l]], axis=-1).astype(bf16)
        bg = jnp.concatenate([lru_b_a[l].reshape(2, N_LRU_BLOCKS, 1, LRU_BLOCK),
                              lru_b_x[l].reshape(2, N_LRU_BLOCKS, 1, LRU_BLOCK)], axis=-1)
        lru = _lru_group(proj, conv_w[l], conv_b[l], wg, bg, lru_lambda[l])
        att = _dilated_attention(proj, b1, b2, b3)
        x1, hp, top_e, top_w, rank, counts = _out_proj_router(
            lru.reshape(t, D_LRU), att.reshape(t, D_ATTN), x2, norm_lru_g[l], norm_attn_g[l],
            w_out[l].astype(bf16), mod3, ln_mix_g[l], ln_mix_b[l], w_router[l], b_router[l], seq)
        dest, tile_e, next_e, rows_valid, n_used, pad_start, pad_len = _routing_tables(top_e, rank, counts, n_tiles)
        xs = _dispatch(hp, dest, pad_start, pad_len, n_used, n_tiles * EXP_TILE)
        hs = _gate_up(xs, w_gate_up, b_gate_up.reshape(DEPTH, N_EXPERTS, 1, 2 * D_FF), l,
                      tile_e, next_e, rows_valid, n_used)
        ys = _down(hs, w_down, b_down.reshape(DEPTH, N_EXPERTS, 1, d), l, tile_e, next_e, rows_valid, n_used)
        x2 = _combine(ys, dest, top_w, x1, mod3, ln_ffn_g[l], ln_ffn_b[l], seq)
    return x2.reshape(nb, seq, d)
```
